```python
import jax, jax.numpy as jnp
from jax import lax
import numpy as np

D_MODEL = 1024
BATCH = 32
SEQ = 2048
DEPTH = 4

GRID_W = 64
CTX_LEN = 256
Q_BLOCK = 128
ROPE_THETA = 10000.0
NORM_EPS = 1e-6

GQA_HEADS = 8
GQA_KV_HEADS = 2
GQA_HEAD_DIM = 128
MLA_HEADS = 8
MLA_Q_RANK = 768
MLA_KV_RANK = 256
MLA_NOPE_DIM = 128
MLA_ROPE_DIM = 64
MLA_V_DIM = 128
N_EXPERTS = 16
EXPERT_FF = 1024
CAPACITY_FACTOR = 2

N_MIXERS = 2
N_GQA_LAYERS = (DEPTH + 1) // 2
N_MLA_LAYERS = DEPTH // 2
DEEPNORM_ALPHA = (2 * DEPTH) ** 0.25
DEEPNORM_BETA = (8 * DEPTH) ** -0.25
ADA_INIT = 0.5

kernel_name = "hybrid_gqa_mla_ecmoe_diffusion_trunk"


def layer_norm(x, g, b):
    xf = x.astype(jnp.float32)
    mu = jnp.mean(xf, axis=-1, keepdims=True)
    xc = xf - mu
    var = jnp.mean(xc * xc, axis=-1, keepdims=True)
    return (xc * lax.rsqrt(var + NORM_EPS) * g + b).astype(x.dtype)


def rms_norm(x, g):
    xf = x.astype(jnp.float32)
    return (xf * lax.rsqrt(jnp.mean(xf * xf, axis=-1, keepdims=True) + NORM_EPS) * g).astype(x.dtype)


def axial_rope_tables(n, rot_dim, dtype):
    t = jnp.arange(n, dtype=jnp.int32)
    row = (t // GRID_W).astype(jnp.float32)
    col = (t % GRID_W).astype(jnp.float32)
    axis_dim = rot_dim // 2
    freqs = ROPE_THETA ** (-jnp.arange(0, axis_dim, 2, dtype=jnp.float32) / axis_dim)
    ang = jnp.concatenate([row[:, None] * freqs, col[:, None] * freqs], axis=-1)
    return jnp.cos(ang).astype(dtype), jnp.sin(ang).astype(dtype)


def apply_rope(x, cos, sin):
    xp = x.reshape(*x.shape[:-1], -1, 2)
    x0, x1 = xp[..., 0], xp[..., 1]
    return jnp.stack([x0 * cos - x1 * sin, x0 * sin + x1 * cos], axis=-1).reshape(x.shape)


def split_blocks(a):
    b, n = a.shape[:2]
    return a.reshape(b, n // Q_BLOCK, Q_BLOCK, *a.shape[2:]).swapaxes(0, 1)


def merge_blocks(a):
    nb, b, qb = a.shape[:3]
    return a.swapaxes(0, 1).reshape(b, nb * qb, *a.shape[3:])


def modulate(h, shift, scale):
    return h * (1.0 + scale) + shift


def gqa_attention(q, k, v):
    scale = GQA_HEAD_DIM ** -0.5

    def block(qb):
        s = jnp.einsum('bqhgd,bkhd->bhgqk', qb, k).astype(jnp.float32) * scale
        p = jax.nn.softmax(s, axis=-1).astype(v.dtype)
        return jnp.einsum('bhgqk,bkhd->bqhgd', p, v)

    o = merge_blocks(lax.map(block, split_blocks(q)))
    return o.reshape(*o.shape[:2], -1)


def gqa_project(h, w_qkv, q_g, k_g):
    b, n, _ = h.shape
    nq = GQA_HEADS * GQA_HEAD_DIM
    nk = GQA_KV_HEADS * GQA_HEAD_DIM
    qkv = h @ w_qkv
    q = qkv[..., :nq].reshape(b, n, GQA_KV_HEADS, GQA_HEADS // GQA_KV_HEADS, GQA_HEAD_DIM)
    k = qkv[..., nq:nq + nk].reshape(b, n, GQA_KV_HEADS, GQA_HEAD_DIM)
    v = qkv[..., nq + nk:].reshape(b, n, GQA_KV_HEADS, GQA_HEAD_DIM)
    return rms_norm(q, q_g), rms_norm(k, k_g), v


def gqa_mixer(h_lat, h_ctx, w_qkv, q_g, k_g, w_o, with_ctx_out):
    q_l, k_l, v_l = gqa_project(h_lat, w_qkv, q_g, k_g)
    cos, sin = axial_rope_tables(h_lat.shape[1], GQA_HEAD_DIM, h_lat.dtype)
    q_l = apply_rope(q_l, cos[:, None, None, :], sin[:, None, None, :])
    k_l = apply_rope(k_l, cos[:, None, :], sin[:, None, :])
    q_c, k_c, v_c = gqa_project(h_ctx, w_qkv, q_g, k_g)
    k_all = jnp.concatenate([k_c, k_l], axis=1)
    v_all = jnp.concatenate([v_c, v_l], axis=1)
    o_lat = gqa_attention(q_l, k_all, v_all) @ w_o
    o_ctx = gqa_attention(q_c, k_c, v_c) @ w_o if with_ctx_out else None
    return o_lat, o_ctx


def mla_attention(q_nope, q_rope, k_nope, k_rope, v):
    scale = (MLA_NOPE_DIM + MLA_ROPE_DIM) ** -0.5

    def block(qs):
        qn, qr = qs
        s = (jnp.einsum('bqhd,bkhd->bhqk', qn, k_nope)
             + jnp.einsum('bqhr,bkr->bhqk', qr, k_rope)).astype(jnp.float32) * scale
        p = jax.nn.softmax(s, axis=-1).astype(v.dtype)
        return jnp.einsum('bhqk,bkhd->bqhd', p, v)

    o = merge_blocks(lax.map(block, (split_blocks(q_nope), split_blocks(q_rope))))
    return o.reshape(*o.shape[:2], -1)


def mla_project(h, w_dq, q_g, w_uq, w_dkv, kv_g, w_ukv):
    b, n, _ = h.shape
    q = (rms_norm(h @ w_dq, q_g) @ w_uq).reshape(b, n, MLA_HEADS, MLA_NOPE_DIM + MLA_ROPE_DIM)
    q_nope, q_rope = q[..., :MLA_NOPE_DIM], q[..., MLA_NOPE_DIM:]
    ckv = h @ w_dkv
    c_kv, k_rope = ckv[..., :MLA_KV_RANK], ckv[..., MLA_KV_RANK:]
    kv = (rms_norm(c_kv, kv_g) @ w_ukv).reshape(b, n, MLA_HEADS, MLA_NOPE_DIM + MLA_V_DIM)
    k_nope, v = kv[..., :MLA_NOPE_DIM], kv[..., MLA_NOPE_DIM:]
    return q_nope, q_rope, k_nope, k_rope, v


def mla_mixer(h_lat, h_ctx, w_dq, q_g, w_uq, w_dkv, kv_g, w_ukv, w_o, with_ctx_out):
    qn_l, qr_l, kn_l, kr_l, v_l = mla_project(h_lat, w_dq, q_g, w_uq, w_dkv, kv_g, w_ukv)
    cos, sin = axial_rope_tables(h_lat.shape[1], MLA_ROPE_DIM, h_lat.dtype)
    qr_l = apply_rope(qr_l, cos[:, None, :], sin[:, None, :])
    kr_l = apply_rope(kr_l, cos, sin)
    qn_c, qr_c, kn_c, kr_c, v_c = mla_project(h_ctx, w_dq, q_g, w_uq, w_dkv, kv_g, w_ukv)
    kn_all = jnp.concatenate([kn_c, kn_l], axis=1)
    kr_all = jnp.concatenate([kr_c, kr_l], axis=1)
    v_all = jnp.concatenate([v_c, v_l], axis=1)
    o_lat = mla_attention(qn_l, qr_l, kn_all, kr_all, v_all) @ w_o
    o_ctx = mla_attention(qn_c, qr_c, kn_c, kr_c, v_c) @ w_o if with_ctx_out else None
    return o_lat, o_ctx


def ec_moe(h, w_router, w_gate, w_up, w_down):
    b, n, _ = h.shape
    cap = CAPACITY_FACTOR * n // N_EXPERTS
    aff = jax.nn.softmax((h @ w_router).astype(jnp.float32), axis=-1)
    g, idx = lax.top_k(aff.swapaxes(1, 2), cap)
    bidx = jnp.arange(b)[:, None, None]
    xs = h[bidx, idx]
    a = jnp.einsum('becd,edf->becf', xs, w_gate)
    u = jnp.einsum('becd,edf->becf', xs, w_up)
    y = jnp.einsum('becf,efd->becd', jax.nn.silu(a) * u, w_down)
    y = (y * g[..., None].astype(y.dtype)).astype(h.dtype)
    return jnp.zeros_like(h).at[bidx, idx].add(y)


def setup_inputs(seed: int = 0) -> dict:
    key = jax.random.key(seed)
    ks = iter(jax.random.split(key, 32))

    def nrm(shape, scale):
        return jax.random.normal(next(ks), shape, jnp.float32) * scale

    D, L, E, F = D_MODEL, DEPTH, N_EXPERTS, EXPERT_FF
    NG, NM = N_GQA_LAYERS, N_MLA_LAYERS
    gqa_cols = (GQA_HEADS + 2 * GQA_KV_HEADS) * GQA_HEAD_DIM
    return {
        "x": nrm((BATCH, SEQ, D), 1.0),
        "c": nrm((BATCH, D), 1.0),
        "ctx": nrm((BATCH, CTX_LEN, D), 1.0),
        "c_ctx": nrm((D,), 1.0),
        "ada_w": nrm((L, D, 6 * D), ADA_INIT * D ** -0.5),
        "ada_b": nrm((L, 6 * D), 0.02),
        "ln_mix_g": 1.0 + nrm((L, D), 0.02),
        "ln_mix_b": nrm((L, D), 0.02),
        "ln_ffn_g": 1.0 + nrm((L, D), 0.02),
        "ln_ffn_b": nrm((L, D), 0.02),
        "router_w": nrm((L, D, E), D ** -0.5),
        "expert_w_gate": nrm((L, E, D, F), D ** -0.5),
        "expert_w_up": nrm((L, E, D, F), D ** -0.5),
        "expert_w_down": nrm((L, E, F, D), DEEPNORM_BETA * F ** -0.5),
        "gqa_w_qkv": nrm((NG, D, gqa_cols), D ** -0.5),
        "gqa_q_g": 1.0 + nrm((NG, GQA_HEAD_DIM), 0.02),
        "gqa_k_g": 1.0 + nrm((NG, GQA_HEAD_DIM), 0.02),
        "gqa_w_o": nrm((NG, GQA_HEADS * GQA_HEAD_DIM, D), DEEPNORM_BETA * (GQA_HEADS * GQA_HEAD_DIM) ** -0.5),
        "mla_w_dq": nrm((NM, D, MLA_Q_RANK), D ** -0.5),
        "mla_q_g": 1.0 + nrm((NM, MLA_Q_RANK), 0.02),
        "mla_w_uq": nrm((NM, MLA_Q_RANK, MLA_HEADS * (MLA_NOPE_DIM + MLA_ROPE_DIM)), MLA_Q_RANK ** -0.5),
        "mla_w_dkv": nrm((NM, D, MLA_KV_RANK + MLA_ROPE_DIM), D ** -0.5),
        "mla_kv_g": 1.0 + nrm((NM, MLA_KV_RANK), 0.02),
        "mla_w_ukv": nrm((NM, MLA_KV_RANK, MLA_HEADS * (MLA_NOPE_DIM + MLA_V_DIM)), MLA_KV_RANK ** -0.5),
        "mla_w_o": nrm((NM, MLA_HEADS * MLA_V_DIM, D), DEEPNORM_BETA * (MLA_HEADS * MLA_V_DIM) ** -0.5),
    }


def reference(x, c, ctx, c_ctx, ada_w, ada_b, ln_mix_g, ln_mix_b, ln_ffn_g, ln_ffn_b,
              router_w, expert_w_gate, expert_w_up, expert_w_down,
              gqa_w_qkv, gqa_q_g, gqa_k_g, gqa_w_o,
              mla_w_dq, mla_q_g, mla_w_uq, mla_w_dkv, mla_kv_g, mla_w_ukv, mla_w_o):
    silu_c = jax.nn.silu(c)
    silu_cc = jax.nn.silu(c_ctx)
    for i in range(DEPTH):
        last = i == DEPTH - 1
        j = i // N_MIXERS
        mod_l = (silu_c @ ada_w[i] + ada_b[i])[:, None, :]
        mod_c = silu_cc @ ada_w[i] + ada_b[i]
        sh1_l, sc1_l, g1_l, sh2_l, sc2_l, g2_l = jnp.split(mod_l, 6, axis=-1)
        sh1_c, sc1_c, g1_c, sh2_c, sc2_c, g2_c = jnp.split(mod_c, 6, axis=-1)

        h_l = modulate(x, sh1_l, sc1_l)
        h_c = modulate(ctx, sh1_c, sc1_c)
        if i % N_MIXERS == 0:
            o_l, o_c = gqa_mixer(h_l, h_c, gqa_w_qkv[j], gqa_q_g[j], gqa_k_g[j], gqa_w_o[j], not last)
        else:
            o_l, o_c = mla_mixer(h_l, h_c, mla_w_dq[j], mla_q_g[j], mla_w_uq[j], mla_w_dkv[j],
                                 mla_kv_g[j], mla_w_ukv[j], mla_w_o[j], not last)
        x = layer_norm(DEEPNORM_ALPHA * x + g1_l * o_l, ln_mix_g[i], ln_mix_b[i])

        f_l = ec_moe(modulate(x, sh2_l, sc2_l), router_w[i], expert_w_gate[i], expert_w_up[i], expert_w_down[i])
        x = layer_norm(DEEPNORM_ALPHA * x + g2_l * f_l, ln_ffn_g[i], ln_ffn_b[i])

        if not last:
            ctx = layer_norm(DEEPNORM_ALPHA * ctx + g1_c * o_c, ln_mix_g[i], ln_mix_b[i])
            f_c = ec_moe(modulate(ctx, sh2_c, sc2_c), router_w[i], expert_w_gate[i], expert_w_up[i], expert_w_down[i])
            ctx = layer_norm(DEEPNORM_ALPHA * ctx + g2_c * f_c, ln_ffn_g[i], ln_ffn_b[i])
    return x
```

```python
import functools

import jax
import jax.numpy as jnp
import numpy as np
from jax import lax
from jax.experimental import pallas as pl
from jax.experimental.pallas import tpu as pltpu

D_MODEL = 1024
DEPTH = 4
GRID_W = 64
ROPE_THETA = 10000.0
NORM_EPS = 1e-6
GQA_HEADS = 8
GQA_KV_HEADS = 2
GQA_HEAD_DIM = 128
MLA_HEADS = 8
MLA_Q_RANK = 768
MLA_KV_RANK = 256
MLA_NOPE_DIM = 128
MLA_ROPE_DIM = 64
MLA_V_DIM = 128
N_EXPERTS = 16
EXPERT_FF = 1024
CAPACITY_FACTOR = 2
N_MIXERS = 2
DEEPNORM_ALPHA = (2 * DEPTH) ** 0.25
LOG2E = 1.4426950408889634

LANE = 128
MOD_ROWS_PAD = 8
VMEM_LIMIT = 56 * 1024 * 1024

BF = jnp.bfloat16
F32 = jnp.float32

_NT = (((1,), (1,)), ((), ()))
_TN = (((0,), (0,)), ((), ()))


def _cparams(sem):
    return pltpu.CompilerParams(dimension_semantics=sem, vmem_limit_bytes=VMEM_LIMIT)


def _ada_kernel(c_ref, w_ref, b_ref, o_ref):
    c = c_ref[...]
    s = (c * jax.nn.sigmoid(c)).astype(BF)
    o_ref[0] = jnp.dot(s, w_ref[0].astype(BF), preferred_element_type=F32) + b_ref[0]


def _ada_call(cvec, ada_w, ada_b):
    n_layers, d, n_out = ada_w.shape
    rows = cvec.shape[0]
    tn = 1536
    return pl.pallas_call(
        _ada_kernel,
        grid=(n_layers, n_out // tn),
        in_specs=[
            pl.BlockSpec((rows, d), lambda i, j: (0, 0)),
            pl.BlockSpec((1, d, tn), lambda i, j: (i, 0, j)),
            pl.BlockSpec((1, 1, tn), lambda i, j: (i, 0, j)),
        ],
        out_specs=pl.BlockSpec((1, rows, tn), lambda i, j: (i, 0, j)),
        out_shape=jax.ShapeDtypeStruct((n_layers, rows, n_out), F32),
        compiler_params=_cparams(("parallel", "parallel")),
        name="ada_mod",
    )(cvec, ada_w, ada_b.reshape(n_layers, 1, n_out))


def _rope_tables(n, rot_dim):
    t = jnp.arange(n, dtype=jnp.int32)
    row = (t // GRID_W).astype(F32)
    col = (t % GRID_W).astype(F32)
    axis_dim = rot_dim // 2
    freqs = ROPE_THETA ** (-jnp.arange(0, axis_dim, 2, dtype=F32) / axis_dim)
    ang = jnp.concatenate([row[:, None] * freqs, col[:, None] * freqs], axis=-1)
    return jnp.cos(ang), jnp.sin(ang)


def _gqa_proj_kernel(x_ref, sh_ref, sc_ref, wqv_ref, wk_ref, gq_ref, gk_ref,
                     cs_ref, sn_ref, cst_ref, snt_ref,
                     qt_ref, k_ref, vt_ref, *, nctx):
    is_ctx = pl.program_id(0) < nctx
    h = (x_ref[0] * (1.0 + sc_ref[0]) + sh_ref[0]).astype(BF)
    qv = lax.dot_general(wqv_ref[...], h, _NT, preferred_element_type=F32)
    cost = jnp.where(is_ctx, 1.0, cst_ref[...])
    sint = jnp.where(is_ctx, 0.0, snt_ref[...])
    hd, half = GQA_HEAD_DIM, GQA_HEAD_DIM // 2
    for i in range(GQA_HEADS):
        q = qv[i * hd:(i + 1) * hd]
        ms = jnp.mean(q * q, axis=0, keepdims=True)
        qn = q * lax.rsqrt(ms + NORM_EPS) * gq_ref[...]
        x0, x1 = qn[:half], qn[half:]
        qt_ref[0, i * hd:i * hd + half, :] = (x0 * cost - x1 * sint).astype(BF)
        qt_ref[0, i * hd + half:(i + 1) * hd, :] = (x0 * sint + x1 * cost).astype(BF)
    vt_ref[0] = qv[GQA_HEADS * hd:].astype(BF)
    kk = jnp.dot(h, wk_ref[...], preferred_element_type=F32)
    cos = jnp.where(is_ctx, 1.0, cs_ref[...])
    sin = jnp.where(is_ctx, 0.0, sn_ref[...])
    for j in range(GQA_KV_HEADS):
        k = kk[:, j * hd:(j + 1) * hd]
        ms = jnp.mean(k * k, axis=1, keepdims=True)
        kn = k * lax.rsqrt(ms + NORM_EPS) * gk_ref[...]
        k_ref[0, :, j * hd:(j + 1) * hd] = (kn * cos + pltpu.roll(kn, half, 1) * sin).astype(BF)


def _mod_spec(layer, chunk, nctx, rows_per_layer, d):
    def imap(g, *_):
        return (layer * rows_per_layer + jnp.where(g < nctx, 0, g - nctx + 1), 0, chunk)
    return pl.BlockSpec((1, 1, d), imap)


def _gqa_proj_call(x, mod, layer, nctx, rows_per_layer, w_qkv, q_g, k_g, tm=512):
    g_n, n, d = x.shape
    hd, half = GQA_HEAD_DIM, GQA_HEAD_DIM // 2
    nq, nk = GQA_HEADS * hd, GQA_KV_HEADS * hd
    perm = np.concatenate([np.arange(0, hd, 2), np.arange(1, hd, 2)])
    wq = w_qkv[:, :nq].reshape(d, GQA_HEADS, hd)[:, :, perm].reshape(d, nq)
    wk = w_qkv[:, nq:nq + nk].reshape(d, GQA_KV_HEADS, hd)[:, :, perm].reshape(d, nk)
    wv = w_qkv[:, nq + nk:]
    wqv_t = jnp.concatenate([wq, wv], axis=1).T.astype(BF)
    wk = wk.astype(BF)
    gq = (q_g[perm] * (hd ** -0.5 * LOG2E)).reshape(hd, 1)
    gk = k_g[perm].reshape(1, hd)
    cos, sin = _rope_tables(n, hd)
    cs = jnp.concatenate([cos, cos], axis=1)
    sn = jnp.concatenate([-sin, sin], axis=1)
    kernel = functools.partial(_gqa_proj_kernel, nctx=nctx)
    full = lambda shape: pl.BlockSpec(shape, lambda g, r: (0,) * len(shape))
    return pl.pallas_call(
        kernel,
        grid=(g_n, n // tm),
        in_specs=[
            pl.BlockSpec((1, tm, d), lambda g, r: (g, r, 0)),
            _mod_spec(layer, 0, nctx, rows_per_layer, d),
            _mod_spec(layer, 1, nctx, rows_per_layer, d),
            full((nq + nk, d)),
            full((d, nk)),
            full((hd, 1)),
            full((1, hd)),
            pl.BlockSpec((tm, hd), lambda g, r: (r, 0)),
            pl.BlockSpec((tm, hd), lambda g, r: (r, 0)),
            pl.BlockSpec((half, tm), lambda g, r: (0, r)),
            pl.BlockSpec((half, tm), lambda g, r: (0, r)),
        ],
        out_specs=[
            pl.BlockSpec((1, nq, tm), lambda g, r: (g, 0, r)),
            pl.BlockSpec((1, tm, nk), lambda g, r: (g, r, 0)),
            pl.BlockSpec((1, nk, tm), lambda g, r: (g, 0, r)),
        ],
        out_shape=[
            jax.ShapeDtypeStruct((g_n, nq, n), BF),
            jax.ShapeDtypeStruct((g_n, n, nk), BF),
            jax.ShapeDtypeStruct((g_n, nk, n), BF),
        ],
        compiler_params=_cparams(("parallel", "parallel")),
        name="gqa_proj",
    )(x, mod, mod, wqv_t, wk, gq, gk, cs, sn, cos.T, sin.T)


MLA_QK_PAD = 256


def _mla_proj_kernel(x_ref, sh_ref, sc_ref, wdq_ref, gq_ref, wuq_ref, wdkv_ref, gkv_ref,
                     wukn_ref, wuv_ref, cs_ref, sn_ref, cst_ref, snt_ref,
                     qt_ref, kn_ref, kr_ref, vt_ref, *, nctx):
    is_ctx = pl.program_id(0) < nctx
    h = (x_ref[0] * (1.0 + sc_ref[0]) + sh_ref[0]).astype(BF)
    cq = jnp.dot(h, wdq_ref[...], preferred_element_type=F32)
    ms = jnp.mean(cq * cq, axis=1, keepdims=True)
    cqn = (cq * lax.rsqrt(ms + NORM_EPS) * gq_ref[...]).astype(BF)
    qt = lax.dot_general(wuq_ref[...], cqn, _NT, preferred_element_type=F32)
    qt = qt * ((MLA_NOPE_DIM + MLA_ROPE_DIM) ** -0.5 * LOG2E)
    cost = jnp.where(is_ctx, 1.0, cst_ref[...])
    sint = jnp.where(is_ctx, 0.0, snt_ref[...])
    hr = MLA_ROPE_DIM // 2
    for i in range(MLA_HEADS):
        base = i * MLA_QK_PAD
        qt_ref[0, base:base + MLA_NOPE_DIM, :] = qt[base:base + MLA_NOPE_DIM].astype(BF)
        r0 = base + MLA_NOPE_DIM
        x0 = qt[r0:r0 + hr]
        x1 = qt[r0 + 2 * hr:r0 + 3 * hr]
        zero = jnp.zeros_like(x0).astype(BF)
        qt_ref[0, r0:r0 + hr, :] = (x0 * cost - x1 * sint).astype(BF)
        qt_ref[0, r0 + hr:r0 + 2 * hr, :] = zero
        qt_ref[0, r0 + 2 * hr:r0 + 3 * hr, :] = (x0 * sint + x1 * cost).astype(BF)
        qt_ref[0, r0 + 3 * hr:r0 + 4 * hr, :] = zero
    ckv = jnp.dot(h, wdkv_ref[...], preferred_element_type=F32)
    c = ckv[:, :MLA_KV_RANK]
    ms = jnp.mean(c * c, axis=1, keepdims=True)
    cn = (c * lax.rsqrt(ms + NORM_EPS) * gkv_ref[...]).astype(BF)
    kr = ckv[:, MLA_KV_RANK:]
    cos = jnp.where(is_ctx, 1.0, cs_ref[...])
    sin = jnp.where(is_ctx, 0.0, sn_ref[...])
    kr_ref[0] = (kr * cos + pltpu.roll(kr, LANE // 2, 1) * sin).astype(BF)
    kn_ref[0] = jnp.dot(cn, wukn_ref[...], preferred_element_type=F32).astype(BF)
    vt_ref[0] = lax.dot_general(wuv_ref[...], cn, _NT, preferred_element_type=F32).astype(BF)


def _mla_proj_call(x, mod, layer, nctx, rows_per_layer, w_dq, q_g, w_uq, w_dkv, kv_g, w_ukv, tm=512):
    g_n, n, d = x.shape
    nh, dn, dr, dv = MLA_HEADS, MLA_NOPE_DIM, MLA_ROPE_DIM, MLA_V_DIM
    hr = dr // 2
    ev, od = np.arange(0, dr, 2), np.arange(1, dr, 2)
    wu = w_uq.reshape(MLA_Q_RANK, nh, dn + dr)
    zq = jnp.zeros((MLA_Q_RANK, nh, hr), F32)
    wu = jnp.concatenate([wu[:, :, :dn], wu[:, :, dn + ev], zq, wu[:, :, dn + od], zq], axis=2)
    wuq_t = wu.reshape(MLA_Q_RANK, nh * MLA_QK_PAD).T.astype(BF)
    zk = jnp.zeros((d, hr), F32)
    wdkv = jnp.concatenate([w_dkv[:, :MLA_KV_RANK], w_dkv[:, MLA_KV_RANK + ev], zk,
                            w_dkv[:, MLA_KV_RANK + od], zk], axis=1).astype(BF)
    wkv = w_ukv.reshape(MLA_KV_RANK, nh, dn + dv)
    wukn = wkv[:, :, :dn].reshape(MLA_KV_RANK, nh * dn).astype(BF)
    wuv_t = wkv[:, :, dn:].reshape(MLA_KV_RANK, nh * dv).T.astype(BF)
    cos, sin = _rope_tables(n, dr)
    z = jnp.zeros_like(cos)
    cs = jnp.concatenate([cos, z, cos, z], axis=1)
    sn = jnp.concatenate([-sin, z, sin, z], axis=1)
    kernel = functools.partial(_mla_proj_kernel, nctx=nctx)
    full = lambda shape: pl.BlockSpec(shape, lambda g, r: (0,) * len(shape))
    return pl.pallas_call(
        kernel,
        grid=(g_n, n // tm),
        in_specs=[
            pl.BlockSpec((1, tm, d), lambda g, r: (g, r, 0)),
            _mod_spec(layer, 0, nctx, rows_per_layer, d),
            _mod_spec(layer, 1, nctx, rows_per_layer, d),
            full((d, MLA_Q_RANK)),
            full((1, MLA_Q_RANK)),
            full((nh * MLA_QK_PAD, MLA_Q_RANK)),
            full((d, MLA_KV_RANK + LANE)),
            full((1, MLA_KV_RANK)),
            full((MLA_KV_RANK, nh * dn)),
            full((nh * dv, MLA_KV_RANK)),
            pl.BlockSpec((tm, LANE), lambda g, r: (r, 0)),
            pl.BlockSpec((tm, LANE), lambda g, r: (r, 0)),
            pl.BlockSpec((hr, tm), lambda g, r: (0, r)),
            pl.BlockSpec((hr, tm), lambda g, r: (0, r)),
        ],
        out_specs=[
            pl.BlockSpec((1, nh * MLA_QK_PAD, tm), lambda g, r: (g, 0, r)),
            pl.BlockSpec((1, tm, nh * dn), lambda g, r: (g, r, 0)),
            pl.BlockSpec((1, tm, LANE), lambda g, r: (g, r, 0)),
            pl.BlockSpec((1, nh * dv, tm), lambda g, r: (g, 0, r)),
        ],
        out_shape=[
            jax.ShapeDtypeStruct((g_n, nh * MLA_QK_PAD, n), BF),
            jax.ShapeDtypeStruct((g_n, n, nh * dn), BF),
            jax.ShapeDtypeStruct((g_n, n, LANE), BF),
            jax.ShapeDtypeStruct((g_n, nh * dv, n), BF),
        ],
        compiler_params=_cparams(("parallel", "parallel")),
        name="mla_proj",
    )(x, mod, mod, w_dq.astype(BF), q_g.reshape(1, -1), wuq_t, wdkv, kv_g.reshape(1, -1),
      wukn, wuv_t, cs, sn, cos.T, sin.T)


def _softmax_pv(scores, values):
    m = functools.reduce(jnp.maximum, [jnp.max(s, axis=0, keepdims=True) for s in scores])
    ps = [jnp.exp2(s - m) for s in scores]
    l = functools.reduce(jnp.add, [jnp.sum(p, axis=0, keepdims=True) for p in ps])
    o = functools.reduce(jnp.add, [jnp.dot(v, p.astype(BF), preferred_element_type=F32)
                                   for v, p in zip(values, ps)])
    return o / l


def _attn_lat_kernel(*refs, mla):
    if mla:
        qt_ref, kl_ref, kc_ref, krl_ref, krc_ref, vtl_ref, vtc_ref, ot_ref = refs
        kl = jnp.concatenate([kl_ref[0], krl_ref[0]], axis=1)
        kc = jnp.concatenate([kc_ref[0], krc_ref[0]], axis=1)
    else:
        qt_ref, kl_ref, kc_ref, vtl_ref, vtc_ref, ot_ref = refs
        kl, kc = kl_ref[0], kc_ref[0]
    qt = qt_ref[0]
    s_l = jnp.dot(kl, qt, preferred_element_type=F32)
    s_c = jnp.dot(kc, qt, preferred_element_type=F32)
    ot_ref[0] = _softmax_pv([s_l, s_c], [vtl_ref[0], vtc_ref[0]]).astype(BF)


def _attn_ctx_kernel(*refs, mla):
    if mla:
        qt_ref, kc_ref, krc_ref, vtc_ref, ot_ref = refs
        kc = jnp.concatenate([kc_ref[0], krc_ref[0]], axis=1)
    else:
        qt_ref, kc_ref, vtc_ref, ot_ref = refs
        kc = kc_ref[0]
    s_c = jnp.dot(kc, qt_ref[0], preferred_element_type=F32)
    ot_ref[0] = _softmax_pv([s_c], [vtc_ref[0]]).astype(BF)


def _attn_call(qt, k, kr, vt, nctx, n_ctx_len, n_heads, n_kv_heads, with_ctx_out, tq=256):
    g_n, _, n = qt.shape
    dq = qt.shape[1] // n_heads
    dk = LANE
    dv = vt.shape[1] // n_kv_heads
    batch = g_n - nctx
    spg = n // n_ctx_len
    rep = n_heads // n_kv_heads
    mla = kr is not None
    ot_shape = jax.ShapeDtypeStruct((g_n, n_heads * dv, n), BF)

    lat_in = [pl.BlockSpec((1, dq, tq), lambda b, h, t: (nctx + b, h, t)),
              pl.BlockSpec((1, n, dk), lambda b, h, t: (nctx + b, 0, h // rep)),
              pl.BlockSpec((1, n_ctx_len, dk), lambda b, h, t: (b // spg, b % spg, h // rep))]
    lat_args = [qt, k, k]
    if mla:
        lat_in += [pl.BlockSpec((1, n, LANE), lambda b, h, t: (nctx + b, 0, 0)),
                   pl.BlockSpec((1, n_ctx_len, LANE), lambda b, h, t: (b // spg, b % spg, 0))]
        lat_args += [kr, kr]
    lat_in += [pl.BlockSpec((1, dv, n), lambda b, h, t: (nctx + b, h // rep, 0)),
               pl.BlockSpec((1, dv, n_ctx_len), lambda b, h, t: (b // spg, h // rep, b % spg))]
    lat_args += [vt, vt]
    ot = pl.pallas_call(
        functools.partial(_attn_lat_kernel, mla=mla),
        grid=(batch, n_heads, n // tq),
        in_specs=lat_in,
        out_specs=pl.BlockSpec((1, dv, tq), lambda b, h, t: (nctx + b, h, t)),
        out_shape=ot_shape,
        compiler_params=_cparams(("parallel", "parallel", "parallel")),
        name="attn_latent",
    )(*lat_args)
    if not with_ctx_out:
        return ot

    ctx_in = [pl.BlockSpec((1, dq, n_ctx_len), lambda b, h: (b // spg, h, b % spg)),
              pl.BlockSpec((1, n_ctx_len, dk), lambda b, h: (b // spg, b % spg, h // rep))]
    ctx_args = [qt, k]
    if mla:
        ctx_in += [pl.BlockSpec((1, n_ctx_len, LANE), lambda b, h: (b // spg, b % spg, 0))]
        ctx_args += [kr]
    ctx_in += [pl.BlockSpec((1, dv, n_ctx_len), lambda b, h: (b // spg, h // rep, b % spg)),
               pl.BlockSpec(memory_space=pl.ANY)]
    ctx_args += [vt, ot]

    def ctx_kernel(*refs):
        _attn_ctx_kernel(*refs[:-2], refs[-1], mla=mla)

    return pl.pallas_call(
        ctx_kernel,
        grid=(batch, n_heads),
        in_specs=ctx_in,
        out_specs=pl.BlockSpec((1, dv, n_ctx_len), lambda b, h: (b // spg, h, b % spg)),
        out_shape=ot_shape,
        input_output_aliases={len(ctx_args) - 1: 0},
        compiler_params=_cparams(("parallel", "parallel")),
        name="attn_ctx",
    )(*ctx_args)


def _layer_norm(z, g, b):
    mu = jnp.mean(z, axis=-1, keepdims=True)
    zc = z - mu
    var = jnp.mean(zc * zc, axis=-1, keepdims=True)
    return zc * lax.rsqrt(var + NORM_EPS) * g + b


def _oproj_kernel(ot_ref, x_ref, wo_ref, g1_ref, lg_ref, lb_ref, sh2_ref, sc2_ref,
                  wrh_ref, wrl_ref, x1_ref, h2_ref, aff_ref):
    y = lax.dot_general(ot_ref[0], wo_ref[...], _TN, preferred_element_type=F32)
    x1 = _layer_norm(DEEPNORM_ALPHA * x_ref[0] + g1_ref[0] * y, lg_ref[...], lb_ref[...])
    x1_ref[0] = x1
    h2 = x1 * (1.0 + sc2_ref[0]) + sh2_ref[0]
    hb = h2.astype(BF)
    h2_ref[0] = hb
    hl = (h2 - hb.astype(F32)).astype(BF)
    lg = (lax.dot_general(wrh_ref[...], hb, _NT, preferred_element_type=F32)
          + lax.dot_general(wrl_ref[...], hb, _NT, preferred_element_type=F32)
          + lax.dot_general(wrh_ref[...], hl, _NT, preferred_element_type=F32))
    e = jnp.exp(lg - jnp.max(lg, axis=0, keepdims=True))
    aff_ref[0] = e / jnp.sum(e, axis=0, keepdims=True)


def _oproj_call(ot, x, mod, layer, nctx, rows_per_layer, w_o, ln_g, ln_b, w_router, g_lo, tm=512):
    g_n, n, d = x.shape
    n_e = w_router.shape[1]
    wr_t = w_router.T
    wrh = wr_t.astype(BF)
    wrl = (wr_t - wrh.astype(F32)).astype(BF)
    full = lambda shape: pl.BlockSpec(shape, lambda g, r: (0,) * len(shape))

    def mspec(chunk):
        def imap(g, r):
            gg = g + g_lo
            return (layer * rows_per_layer + jnp.where(gg < nctx, 0, gg - nctx + 1), 0, chunk)
        return pl.BlockSpec((1, 1, d), imap)

    return pl.pallas_call(
        _oproj_kernel,
        grid=(g_n - g_lo, n // tm),
        in_specs=[
            pl.BlockSpec((1, ot.shape[1], tm), lambda g, r: (g + g_lo, 0, r)),
            pl.BlockSpec((1, tm, d), lambda g, r: (g + g_lo, r, 0)),
            full(w_o.shape),
            mspec(2),
            full((1, d)), full((1, d)),
            mspec(3), mspec(4),
            full((n_e, d)), full((n_e, d)),
        ],
        out_specs=[
            pl.BlockSpec((1, tm, d), lambda g, r: (g, r, 0)),
            pl.BlockSpec((1, tm, d), lambda g, r: (g, r, 0)),
            pl.BlockSpec((1, n_e, tm), lambda g, r: (g, 0, r)),
        ],
        out_shape=[
            jax.ShapeDtypeStruct((g_n - g_lo, n, d), F32),
            jax.ShapeDtypeStruct((g_n - g_lo, n, d), BF),
            jax.ShapeDtypeStruct((g_n - g_lo, n_e, n), F32),
        ],
        compiler_params=_cparams(("parallel", "parallel")),
        name="oproj_norm_router",
    )(ot, x, w_o.astype(BF), mod, ln_g.reshape(1, d), ln_b.reshape(1, d), mod, mod, wrh, wrl)


def _route_segments(aff, seg_len, cap, tri):
    n_e, n = aff.shape
    nseg = n // seg_len
    bits = pltpu.bitcast(aff, jnp.int32)
    lane = lax.broadcasted_iota(jnp.int32, (n_e, n), 1)
    assert seg_len & (seg_len - 1) == 0
    pos = lane & (seg_len - 1)

    def seg_count(mask):
        v = jnp.where(mask, 1.0, 0.0)
        parts = []
        for s in range(nseg):
            c = jnp.sum(v[:, s * seg_len:(s + 1) * seg_len], axis=1, keepdims=True)
            parts.append(jnp.broadcast_to(c, (n_e, seg_len)))
        return parts[0] if nseg == 1 else jnp.concatenate(parts, axis=1)

    def thr_step(i, thr):
        cand = thr | (jnp.int32(1) << (30 - i))
        return jnp.where(seg_count(bits >= cand) >= cap, cand, thr)

    thr = lax.fori_loop(0, 31, thr_step, jnp.zeros((n_e, n), jnp.int32))
    gt = bits > thr
    eq = bits == thr
    need = cap - seg_count(gt)

    def tie_step(i, j):
        cand = j | (jnp.int32(1) << (seg_len.bit_length() - 2 - i))
        return jnp.where(seg_count(eq & (pos < cand)) < need, cand, j)

    j = lax.fori_loop(0, seg_len.bit_length() - 1, tie_step, jnp.zeros((n_e, n), jnp.int32))
    sel = gt | (eq & (pos <= j))

    selb = jnp.where(sel, 1.0, 0.0).astype(BF)
    tiles_per_seg = seg_len // LANE
    out = []
    run = jnp.zeros((n_e, 1), F32)
    for t in range(n // LANE):
        if t % tiles_per_seg == 0:
            run = jnp.full((n_e, 1), float((t // tiles_per_seg) * cap), F32)
        inc = jnp.dot(selb[:, t * LANE:(t + 1) * LANE], tri, preferred_element_type=F32)
        out.append(inc + (run - 1.0))
        run = run + inc[:, LANE - 1:LANE]
    slot = jnp.concatenate(out, axis=1).astype(jnp.int32)
    return jnp.where(sel, slot, -1)


def _route_kernel(aff_ref, tri_ref, slot_ref, *, nctx, n_ctx_len, g_lo):
    n = aff_ref.shape[2]
    g = pl.program_id(0) + g_lo

    if g_lo < nctx:
        @pl.when(g < nctx)
        def _():
            slot_ref[0] = _route_segments(aff_ref[0], n_ctx_len,
                                          CAPACITY_FACTOR * n_ctx_len // N_EXPERTS, tri_ref[...])

    @pl.when(g >= nctx)
    def _():
        slot_ref[0] = _route_segments(aff_ref[0], n, CAPACITY_FACTOR * n // N_EXPERTS, tri_ref[...])


def _route_call(aff, nctx, n_ctx_len, g_lo):
    g_n, n_e, n = aff.shape
    tri = jnp.triu(jnp.ones((LANE, LANE), F32)).astype(BF)
    return pl.pallas_call(
        functools.partial(_route_kernel, nctx=nctx, n_ctx_len=n_ctx_len, g_lo=g_lo),
        grid=(g_n,),
        in_specs=[pl.BlockSpec((1, n_e, n), lambda g: (g, 0, 0)),
                  pl.BlockSpec((LANE, LANE), lambda g: (0, 0))],
        out_specs=pl.BlockSpec((1, n_e, n), lambda g: (g, 0, 0)),
        out_shape=jax.ShapeDtypeStruct((g_n, n_e, n), jnp.int32),
        compiler_params=_cparams(("parallel",)),
        name="ec_route",
    )(aff, tri)


def _moe_kernel(h_ref, slot_ref, gate_ref, wg_ref, wu_ref, wd_ref, f_ref, *, n_slots, chunk):
    e = pl.program_id(1)
    n = h_ref.shape[1]

    @pl.when(e == 0)
    def _():
        f_ref[...] = jnp.zeros_like(f_ref)

    slot = slot_ref[0]
    hit = lax.broadcasted_iota(jnp.int32, (n_slots, n), 0) == slot
    sel = jnp.where(hit, 1.0, 0.0).astype(BF)
    selg = jnp.where(hit, gate_ref[0], 0.0).astype(BF)
    xs = jnp.dot(sel, h_ref[0], preferred_element_type=F32).astype(BF)
    a = jnp.dot(xs, wg_ref[0], preferred_element_type=F32)
    u = jnp.dot(xs, wu_ref[0], preferred_element_type=F32)
    hh = (a * jax.nn.sigmoid(a) * u).astype(BF)
    y = jnp.dot(hh, wd_ref[0], preferred_element_type=F32).astype(BF)
    for c in range(n // chunk):
        rows = slice(c * chunk, (c + 1) * chunk)
        f_ref[0, rows, :] += lax.dot_general(selg[:, rows], y, _TN, preferred_element_type=F32)


def _moe_call(h2, slot, aff, w_gate, w_up, w_down, n_slots):
    g_n, n, d = h2.shape
    n_e, _, ff = w_gate.shape
    slot3 = slot.reshape(g_n * n_e, 1, n)
    gate3 = aff.reshape(g_n * n_e, 1, n)
    return pl.pallas_call(
        functools.partial(_moe_kernel, n_slots=n_slots, chunk=256),
        grid=(g_n, n_e),
        in_specs=[
            pl.BlockSpec((1, n, d), lambda g, e: (g, 0, 0)),
            pl.BlockSpec((1, 1, n), lambda g, e: (g * n_e + e, 0, 0)),
            pl.BlockSpec((1, 1, n), lambda g, e: (g * n_e + e, 0, 0)),
            pl.BlockSpec((1, d, ff), lambda g, e: (e, 0, 0)),
            pl.BlockSpec((1, d, ff), lambda g, e: (e, 0, 0)),
            pl.BlockSpec((1, ff, d), lambda g, e: (e, 0, 0)),
        ],
        out_specs=pl.BlockSpec((1, n, d), lambda g, e: (g, 0, 0)),
        out_shape=jax.ShapeDtypeStruct((g_n, n, d), F32),
        compiler_params=_cparams(("parallel", "arbitrary")),
        name="ec_moe",
    )(h2, slot3, gate3, w_gate, w_up, w_down)


def _ffn_norm_kernel(x1_ref, f_ref, g2_ref, lg_ref, lb_ref, x2_ref):
    x2_ref[0] = _layer_norm(DEEPNORM_ALPHA * x1_ref[0] + g2_ref[0] * f_ref[0],
                            lg_ref[...], lb_ref[...])


def _ffn_norm_call(x1, f, mod, layer, nctx, rows_per_layer, ln_g, ln_b, g_lo, tm=512):
    g_n, n, d = x1.shape

    def imap(g, r):
        gg = g + g_lo
        return (layer * rows_per_layer + jnp.where(gg < nctx, 0, gg - nctx + 1), 0, 5)

    full = lambda shape: pl.BlockSpec(shape, lambda g, r: (0,) * len(shape))
    return pl.pallas_call(
        _ffn_norm_kernel,
        grid=(g_n, n // tm),
        in_specs=[
            pl.BlockSpec((1, tm, d), lambda g, r: (g, r, 0)),
            pl.BlockSpec((1, tm, d), lambda g, r: (g, r, 0)),
            pl.BlockSpec((1, 1, d), imap),
            full((1, d)), full((1, d)),
        ],
        out_specs=pl.BlockSpec((1, tm, d), lambda g, r: (g, r, 0)),
        out_shape=jax.ShapeDtypeStruct((g_n, n, d), F32),
        compiler_params=_cparams(("parallel", "parallel")),
        name="ffn_norm",
    )(x1, f, mod, ln_g.reshape(1, d), ln_b.reshape(1, d))


def kernel(x, c, ctx, c_ctx, ada_w, ada_b, ln_mix_g, ln_mix_b, ln_ffn_g, ln_ffn_b, router_w, expert_w_gate, expert_w_up, expert_w_down, gqa_w_qkv, gqa_q_g, gqa_k_g, gqa_w_o, mla_w_dq, mla_q_g, mla_w_uq, mla_w_dkv, mla_kv_g, mla_w_ukv, mla_w_o):
    batch, n, d = x.shape
    n_ctx_len = ctx.shape[1]
    depth = ada_w.shape[0]
    assert n % n_ctx_len == 0 and (batch * n_ctx_len) % n == 0
    nctx = batch * n_ctx_len // n
    n_slots = CAPACITY_FACTOR * n // N_EXPERTS

    rows = -(-(batch + 1) // MOD_ROWS_PAD) * MOD_ROWS_PAD
    cvec = jnp.concatenate([c_ctx[None, :], c, jnp.zeros((rows - batch - 1, d), F32)], axis=0)
    mod = _ada_call(cvec, ada_w, ada_b).reshape(depth * rows, 1, 6 * d)

    xs = jnp.concatenate([ctx.reshape(nctx, n, d), x], axis=0)
    for i in range(depth):
        last = i == depth - 1
        j = i // N_MIXERS
        if i % N_MIXERS == 0:
            qt, k, vt = _gqa_proj_call(xs, mod, i, nctx, rows, gqa_w_qkv[j], gqa_q_g[j], gqa_k_g[j])
            ot = _attn_call(qt, k, None, vt, nctx, n_ctx_len, GQA_HEADS, GQA_KV_HEADS, not last)
            w_o = gqa_w_o[j]
        else:
            qt, kn, kr, vt = _mla_proj_call(xs, mod, i, nctx, rows, mla_w_dq[j], mla_q_g[j], mla_w_uq[j],
                                            mla_w_dkv[j], mla_kv_g[j], mla_w_ukv[j])
            ot = _attn_call(qt, kn, kr, vt, nctx, n_ctx_len, MLA_HEADS, MLA_HEADS, not last)
            w_o = mla_w_o[j]
        g_lo = nctx if last else 0
        x1, h2, aff = _oproj_call(ot, xs, mod, i, nctx, rows, w_o, ln_mix_g[i], ln_mix_b[i],
                                  router_w[i], g_lo)
        slot = _route_call(aff, nctx, n_ctx_len, g_lo)
        f = _moe_call(h2, slot, aff, expert_w_gate[i].astype(BF), expert_w_up[i].astype(BF),
                      expert_w_down[i].astype(BF), n_slots)
        xs = _ffn_norm_call(x1, f, mod, i, nctx, rows, ln_ffn_g[i], ln_ffn_b[i], g_lo)
    return xs
```

```python
import functools

import jax
import jax.numpy as jnp
import numpy as np
from jax import lax
from jax.experimental import pallas as pl
from jax.experimental.pallas import tpu as pltpu

D_MODEL = 1024
DEPTH = 4
GRID_W = 64
ROPE_THETA = 10000.0
NORM_EPS = 1e-6
GQA_HEADS = 8
GQA_KV_HEADS = 2
GQA_HEAD_DIM = 128
MLA_HEADS = 8
MLA_Q_RANK = 768
MLA_KV_RANK = 256
MLA_NOPE_DIM = 128
MLA_ROPE_DIM = 64
MLA_V_DIM = 128
N_EXPERTS = 16
EXPERT_FF = 1024
CAPACITY_FACTOR = 2
N_MIXERS = 2
DEEPNORM_ALPHA = (2 * DEPTH) ** 0.25
LOG2E = 1.4426950408889634

LANE = 128
MOD_ROWS_PAD = 8
VMEM_LIMIT = 56 * 1024 * 1024

BF = jnp.bfloat16
F32 = jnp.float32

_NT = (((1,), (1,)), ((), ()))
_TN = (((0,), (0,)), ((), ()))


def _cparams(sem):
    return pltpu.CompilerParams(dimension_semantics=sem, vmem_limit_bytes=VMEM_LIMIT)


def _ada_kernel(c_ref, w_ref, b_ref, o_ref):
    c = c_ref[...]
    s = (c * jax.nn.sigmoid(c)).astype(BF)
    o_ref[0] = jnp.dot(s, w_ref[0].astype(BF), preferred_element_type=F32) + b_ref[0]


def _ada_call(cvec, ada_w, ada_b):
    n_layers, d, n_out = ada_w.shape
    rows = cvec.shape[0]
    tn = 1536
    return pl.pallas_call(
        _ada_kernel,
        grid=(n_layers, n_out // tn),
        in_specs=[
            pl.BlockSpec((rows, d), lambda i, j: (0, 0)),
            pl.BlockSpec((1, d, tn), lambda i, j: (i, 0, j)),
            pl.BlockSpec((1, 1, tn), lambda i, j: (i, 0, j)),
        ],
        out_specs=pl.BlockSpec((1, rows, tn), lambda i, j: (i, 0, j)),
        out_shape=jax.ShapeDtypeStruct((n_layers, rows, n_out), F32),
        compiler_params=_cparams(("parallel", "parallel")),
        name="ada_mod",
    )(cvec, ada_w, ada_b.reshape(n_layers, 1, n_out))


def _rope_tables(n, rot_dim):
    t = jnp.arange(n, dtype=jnp.int32)
    row = (t // GRID_W).astype(F32)
    col = (t % GRID_W).astype(F32)
    axis_dim = rot_dim // 2
    freqs = ROPE_THETA ** (-jnp.arange(0, axis_dim, 2, dtype=F32) / axis_dim)
    ang = jnp.concatenate([row[:, None] * freqs, col[:, None] * freqs], axis=-1)
    return jnp.cos(ang), jnp.sin(ang)


def _gqa_proj_kernel(x_ref, sh_ref, sc_ref, wqv_ref, wk_ref, gq_ref, gk_ref,
                     cs_ref, sn_ref, cst_ref, snt_ref,
                     qt_ref, k_ref, vt_ref, *, nctx):
    is_ctx = pl.program_id(0) < nctx
    h = (x_ref[0] * (1.0 + sc_ref[0]) + sh_ref[0]).astype(BF)
    qv = lax.dot_general(wqv_ref[...], h, _NT, preferred_element_type=F32)
    cost = jnp.where(is_ctx, 1.0, cst_ref[...])
    sint = jnp.where(is_ctx, 0.0, snt_ref[...])
    hd, half = GQA_HEAD_DIM, GQA_HEAD_DIM // 2
    for i in range(GQA_HEADS):
        q = qv[i * hd:(i + 1) * hd]
        ms = jnp.mean(q * q, axis=0, keepdims=True)
        qn = q * lax.rsqrt(ms + NORM_EPS) * gq_ref[...]
        x0, x1 = qn[:half], qn[half:]
        qt_ref[0, i * hd:i * hd + half, :] = (x0 * cost - x1 * sint).astype(BF)
        qt_ref[0, i * hd + half:(i + 1) * hd, :] = (x0 * sint + x1 * cost).astype(BF)
    vt_ref[0] = qv[GQA_HEADS * hd:].astype(BF)
    kk = jnp.dot(h, wk_ref[...], preferred_element_type=F32)
    cos = jnp.where(is_ctx, 1.0, cs_ref[...])
    sin = jnp.where(is_ctx, 0.0, sn_ref[...])
    for j in range(GQA_KV_HEADS):
        k = kk[:, j * hd:(j + 1) * hd]
        ms = jnp.mean(k * k, axis=1, keepdims=True)
        kn = k * lax.rsqrt(ms + NORM_EPS) * gk_ref[...]
        k_ref[0, :, j * hd:(j + 1) * hd] = (kn * cos + pltpu.roll(kn, half, 1) * sin).astype(BF)


def _mod_spec(layer, chunk, nctx, rows_per_layer, d):
    def imap(g, *_):
        return (layer * rows_per_layer + jnp.where(g < nctx, 0, g - nctx + 1), 0, chunk)
    return pl.BlockSpec((1, 1, d), imap)


def _gqa_proj_call(x, mod, layer, nctx, rows_per_layer, w_qkv, q_g, k_g, tm=512):
    g_n, n, d = x.shape
    hd, half = GQA_HEAD_DIM, GQA_HEAD_DIM // 2
    nq, nk = GQA_HEADS * hd, GQA_KV_HEADS * hd
    perm = np.concatenate([np.arange(0, hd, 2), np.arange(1, hd, 2)])
    wq = w_qkv[:, :nq].reshape(d, GQA_HEADS, hd)[:, :, perm].reshape(d, nq)
    wk = w_qkv[:, nq:nq + nk].reshape(d, GQA_KV_HEADS, hd)[:, :, perm].reshape(d, nk)
    wv = w_qkv[:, nq + nk:]
    wqv_t = jnp.concatenate([wq, wv], axis=1).T.astype(BF)
    wk = wk.astype(BF)
    gq = (q_g[perm] * (hd ** -0.5 * LOG2E)).reshape(hd, 1)
    gk = k_g[perm].reshape(1, hd)
    cos, sin = _rope_tables(n, hd)
    cs = jnp.concatenate([cos, cos], axis=1)
    sn = jnp.concatenate([-sin, sin], axis=1)
    kernel = functools.partial(_gqa_proj_kernel, nctx=nctx)
    full = lambda shape: pl.BlockSpec(shape, lambda g, r: (0,) * len(shape))
    return pl.pallas_call(
        kernel,
        grid=(g_n, n // tm),
        in_specs=[
            pl.BlockSpec((1, tm, d), lambda g, r: (g, r, 0)),
            _mod_spec(layer, 0, nctx, rows_per_layer, d),
            _mod_spec(layer, 1, nctx, rows_per_layer, d),
            full((nq + nk, d)),
            full((d, nk)),
            full((hd, 1)),
            full((1, hd)),
            pl.BlockSpec((tm, hd), lambda g, r: (r, 0)),
            pl.BlockSpec((tm, hd), lambda g, r: (r, 0)),
            pl.BlockSpec((half, tm), lambda g, r: (0, r)),
            pl.BlockSpec((half, tm), lambda g, r: (0, r)),
        ],
        out_specs=[
            pl.BlockSpec((1, nq, tm), lambda g, r: (g, 0, r)),
            pl.BlockSpec((1, tm, nk), lambda g, r: (g, r, 0)),
            pl.BlockSpec((1, nk, tm), lambda g, r: (g, 0, r)),
        ],
        out_shape=[
            jax.ShapeDtypeStruct((g_n, nq, n), BF),
            jax.ShapeDtypeStruct((g_n, n, nk), BF),
            jax.ShapeDtypeStruct((g_n, nk, n), BF),
        ],
        compiler_params=_cparams(("parallel", "parallel")),
        name="gqa_proj",
    )(x, mod, mod, wqv_t, wk, gq, gk, cs, sn, cos.T, sin.T)


MLA_QK_PAD = 256


def _mla_proj_kernel(x_ref, sh_ref, sc_ref, wdq_ref, gq_ref, wuq_ref, wdkv_ref, gkv_ref,
                     wukn_ref, wuv_ref, cs_ref, sn_ref, cst_ref, snt_ref,
                     qt_ref, kn_ref, kr_ref, vt_ref, *, nctx):
    is_ctx = pl.program_id(0) < nctx
    h = (x_ref[0] * (1.0 + sc_ref[0]) + sh_ref[0]).astype(BF)
    cq = jnp.dot(h, wdq_ref[...], preferred_element_type=F32)
    ms = jnp.mean(cq * cq, axis=1, keepdims=True)
    cqn = (cq * lax.rsqrt(ms + NORM_EPS) * gq_ref[...]).astype(BF)
    qt = lax.dot_general(wuq_ref[...], cqn, _NT, preferred_element_type=F32)
    qt = qt * ((MLA_NOPE_DIM + MLA_ROPE_DIM) ** -0.5 * LOG2E)
    cost = jnp.where(is_ctx, 1.0, cst_ref[...])
    sint = jnp.where(is_ctx, 0.0, snt_ref[...])
    hr = MLA_ROPE_DIM // 2
    for i in range(MLA_HEADS):
        base = i * MLA_QK_PAD
        qt_ref[0, base:base + MLA_NOPE_DIM, :] = qt[base:base + MLA_NOPE_DIM].astype(BF)
        r0 = base + MLA_NOPE_DIM
        x0 = qt[r0:r0 + hr]
        x1 = qt[r0 + 2 * hr:r0 + 3 * hr]
        zero = jnp.zeros_like(x0).astype(BF)
        qt_ref[0, r0:r0 + hr, :] = (x0 * cost - x1 * sint).astype(BF)
        qt_ref[0, r0 + hr:r0 + 2 * hr, :] = zero
        qt_ref[0, r0 + 2 * hr:r0 + 3 * hr, :] = (x0 * sint + x1 * cost).astype(BF)
        qt_ref[0, r0 + 3 * hr:r0 + 4 * hr, :] = zero
    ckv = jnp.dot(h, wdkv_ref[...], preferred_element_type=F32)
    c = ckv[:, :MLA_KV_RANK]
    ms = jnp.mean(c * c, axis=1, keepdims=True)
    cn = (c * lax.rsqrt(ms + NORM_EPS) * gkv_ref[...]).astype(BF)
    kr = ckv[:, MLA_KV_RANK:]
    cos = jnp.where(is_ctx, 1.0, cs_ref[...])
    sin = jnp.where(is_ctx, 0.0, sn_ref[...])
    kr_ref[0] = (kr * cos + pltpu.roll(kr, LANE // 2, 1) * sin).astype(BF)
    kn_ref[0] = jnp.dot(cn, wukn_ref[...], preferred_element_type=F32).astype(BF)
    vt_ref[0] = lax.dot_general(wuv_ref[...], cn, _NT, preferred_element_type=F32).astype(BF)


def _mla_proj_call(x, mod, layer, nctx, rows_per_layer, w_dq, q_g, w_uq, w_dkv, kv_g, w_ukv, tm=512):
    g_n, n, d = x.shape
    nh, dn, dr, dv = MLA_HEADS, MLA_NOPE_DIM, MLA_ROPE_DIM, MLA_V_DIM
    hr = dr // 2
    ev, od = np.arange(0, dr, 2), np.arange(1, dr, 2)
    wu = w_uq.reshape(MLA_Q_RANK, nh, dn + dr)
    zq = jnp.zeros((MLA_Q_RANK, nh, hr), F32)
    wu = jnp.concatenate([wu[:, :, :dn], wu[:, :, dn + ev], zq, wu[:, :, dn + od], zq], axis=2)
    wuq_t = wu.reshape(MLA_Q_RANK, nh * MLA_QK_PAD).T.astype(BF)
    zk = jnp.zeros((d, hr), F32)
    wdkv = jnp.concatenate([w_dkv[:, :MLA_KV_RANK], w_dkv[:, MLA_KV_RANK + ev], zk,
                            w_dkv[:, MLA_KV_RANK + od], zk], axis=1).astype(BF)
    wkv = w_ukv.reshape(MLA_KV_RANK, nh, dn + dv)
    wukn = wkv[:, :, :dn].reshape(MLA_KV_RANK, nh * dn).astype(BF)
    wuv_t = wkv[:, :, dn:].reshape(MLA_KV_RANK, nh * dv).T.astype(BF)
    cos, sin = _rope_tables(n, dr)
    z = jnp.zeros_like(cos)
    cs = jnp.concatenate([cos, z, cos, z], axis=1)
    sn = jnp.concatenate([-sin, z, sin, z], axis=1)
    kernel = functools.partial(_mla_proj_kernel, nctx=nctx)
    full = lambda shape: pl.BlockSpec(shape, lambda g, r: (0,) * len(shape))
    return pl.pallas_call(
        kernel,
        grid=(g_n, n // tm),
        in_specs=[
            pl.BlockSpec((1, tm, d), lambda g, r: (g, r, 0)),
            _mod_spec(layer, 0, nctx, rows_per_layer, d),
            _mod_spec(layer, 1, nctx, rows_per_layer, d),
            full((d, MLA_Q_RANK)),
            full((1, MLA_Q_RANK)),
            full((nh * MLA_QK_PAD, MLA_Q_RANK)),
            full((d, MLA_KV_RANK + LANE)),
            full((1, MLA_KV_RANK)),
            full((MLA_KV_RANK, nh * dn)),
            full((nh * dv, MLA_KV_RANK)),
            pl.BlockSpec((tm, LANE), lambda g, r: (r, 0)),
            pl.BlockSpec((tm, LANE), lambda g, r: (r, 0)),
            pl.BlockSpec((hr, tm), lambda g, r: (0, r)),
            pl.BlockSpec((hr, tm), lambda g, r: (0, r)),
        ],
        out_specs=[
            pl.BlockSpec((1, nh * MLA_QK_PAD, tm), lambda g, r: (g, 0, r)),
            pl.BlockSpec((1, tm, nh * dn), lambda g, r: (g, r, 0)),
            pl.BlockSpec((1, tm, LANE), lambda g, r: (g, r, 0)),
            pl.BlockSpec((1, nh * dv, tm), lambda g, r: (g, 0, r)),
        ],
        out_shape=[
            jax.ShapeDtypeStruct((g_n, nh * MLA_QK_PAD, n), BF),
            jax.ShapeDtypeStruct((g_n, n, nh * dn), BF),
            jax.ShapeDtypeStruct((g_n, n, LANE), BF),
            jax.ShapeDtypeStruct((g_n, nh * dv, n), BF),
        ],
        compiler_params=_cparams(("parallel", "parallel")),
        name="mla_proj",
    )(x, mod, mod, w_dq.astype(BF), q_g.reshape(1, -1), wuq_t, wdkv, kv_g.reshape(1, -1),
      wukn, wuv_t, cs, sn, cos.T, sin.T)


SUBLANE = 8


def _attn_kernel(*refs, mla, with_lat, hb, rep, dq, dv, ck, sb):
    refs = list(refs)
    qt_ref = refs.pop(0)
    kl_ref = refs.pop(0) if with_lat else None
    kc_ref = refs.pop(0)
    krl_ref = refs.pop(0) if (mla and with_lat) else None
    krc_ref = refs.pop(0) if mla else None
    vtl_ref = refs.pop(0) if with_lat else None
    vtc_ref = refs.pop(0)
    p_scr = refs.pop()
    s_scr = refs.pop()
    ot_ref = refs.pop()
    tq = qt_ref.shape[2]
    n_ctx = kc_ref.shape[1]

    chunks = [(kc_ref, krc_ref, vtc_ref, 0, n_ctx, 0)]
    if with_lat:
        for r0 in range(0, kl_ref.shape[1], ck):
            chunks.append((kl_ref, krl_ref, vtl_ref, r0, ck, n_ctx + r0))

    def stage_a(u, chunk):
        k_ref, kr_ref, _, r0, rows, s0 = chunk
        j = u // rep
        keys = k_ref[0, r0:r0 + rows, j * LANE:(j + 1) * LANE]
        if mla:
            keys = jnp.concatenate([keys, kr_ref[0, r0:r0 + rows, :]], axis=1)
        s = jnp.dot(keys, qt_ref[0, u * dq:(u + 1) * dq, :], preferred_element_type=F32)
        s_scr[u % 2, s0:s0 + rows, :] = s
        return jnp.max(s.reshape(rows // SUBLANE, SUBLANE, tq), axis=0)

    def stage_b(u, ci, chunk, m):
        _, _, vt_ref, r0, rows, s0 = chunk
        j = u // rep
        l8 = None
        for r in range(0, rows, sb):
            p = jnp.exp2(s_scr[u % 2, s0 + r:s0 + r + sb, :] - m)
            ps = jnp.sum(p.reshape(sb // SUBLANE, SUBLANE, tq), axis=0)
            l8 = ps if l8 is None else l8 + ps
            p_scr[ci % 2, r:r + sb, :] = p.astype(BF)
        pv = jnp.dot(vt_ref[0, j * dv:(j + 1) * dv, r0:r0 + rows], p_scr[ci % 2, :rows, :],
                     preferred_element_type=F32)
        return l8, pv

    m_cur = None
    for u in range(hb + 1):
        m_parts, l_parts, acc = [], [], None
        for ci, chunk in enumerate(chunks):
            if u < hb:
                m_parts.append(stage_a(u, chunk))
            if u >= 1:
                l8, pv = stage_b(u - 1, ci, chunk, m_cur)
                l_parts.append(l8)
                acc = pv if acc is None else acc + pv
        if u >= 1:
            l = jnp.sum(functools.reduce(jnp.add, l_parts), axis=0, keepdims=True)
            ot_ref[0, (u - 1) * dv:u * dv, :] = (acc / l).astype(BF)
        if u < hb:
            m_cur = jnp.max(functools.reduce(jnp.maximum, m_parts), axis=0, keepdims=True)


def _attn_call(qt, k, kr, vt, nctx, n_ctx_len, n_heads, n_kv_heads, with_ctx_out, tq=512, ck=256, sb=32):
    g_n, _, n = qt.shape
    dq = qt.shape[1] // n_heads
    dv = vt.shape[1] // n_kv_heads
    batch = g_n - nctx
    spg = n // n_ctx_len
    rep = n_heads // n_kv_heads
    hb = n_heads
    hkv = hb // rep
    mla = kr is not None
    ot_shape = jax.ShapeDtypeStruct((g_n, n_heads * dv, n), BF)
    kern = functools.partial(_attn_kernel, mla=mla, hb=hb, rep=rep, dq=dq, dv=dv, ck=ck, sb=sb)

    lat_in = [pl.BlockSpec((1, hb * dq, tq), lambda b, h, t: (nctx + b, h, t)),
              pl.BlockSpec((1, n, hkv * LANE), lambda b, h, t: (nctx + b, 0, h)),
              pl.BlockSpec((1, n_ctx_len, hkv * LANE), lambda b, h, t: (b // spg, b % spg, h))]
    lat_args = [qt, k, k]
    if mla:
        lat_in += [pl.BlockSpec((1, n, LANE), lambda b, h, t: (nctx + b, 0, 0)),
                   pl.BlockSpec((1, n_ctx_len, LANE), lambda b, h, t: (b // spg, b % spg, 0))]
        lat_args += [kr, kr]
    lat_in += [pl.BlockSpec((1, hkv * dv, n), lambda b, h, t: (nctx + b, h, 0)),
               pl.BlockSpec((1, hkv * dv, n_ctx_len), lambda b, h, t: (b // spg, h, b % spg))]
    lat_args += [vt, vt]
    ot = pl.pallas_call(
        functools.partial(kern, with_lat=True),
        grid=(batch, n_heads // hb, n // tq),
        in_specs=lat_in,
        out_specs=pl.BlockSpec((1, hb * dv, tq), lambda b, h, t: (nctx + b, h, t)),
        out_shape=ot_shape,
        scratch_shapes=[pltpu.VMEM((2, n_ctx_len + n, tq), F32),
                        pltpu.VMEM((2, max(ck, n_ctx_len), tq), BF)],
        compiler_params=_cparams(("parallel", "parallel", "parallel")),
        name="attn_latent",
    )(*lat_args)
    if not with_ctx_out:
        return ot

    ctx_in = [pl.BlockSpec((1, hb * dq, n_ctx_len), lambda b, h: (b // spg, h, b % spg)),
              pl.BlockSpec((1, n_ctx_len, hkv * LANE), lambda b, h: (b // spg, b % spg, h))]
    ctx_args = [qt, k]
    if mla:
        ctx_in += [pl.BlockSpec((1, n_ctx_len, LANE), lambda b, h: (b // spg, b % spg, 0))]
        ctx_args += [kr]
    ctx_in += [pl.BlockSpec((1, hkv * dv, n_ctx_len), lambda b, h: (b // spg, h, b % spg)),
               pl.BlockSpec(memory_space=pl.ANY)]
    ctx_args += [vt, ot]
    return pl.pallas_call(
        functools.partial(kern, with_lat=False),
        grid=(batch, n_heads // hb),
        in_specs=ctx_in,
        out_specs=pl.BlockSpec((1, hb * dv, n_ctx_len), lambda b, h: (b // spg, h, b % spg)),
        out_shape=ot_shape,
        input_output_aliases={len(ctx_args) - 1: 0},
        scratch_shapes=[pltpu.VMEM((2, n_ctx_len, n_ctx_len), F32),
                        pltpu.VMEM((2, n_ctx_len, n_ctx_len), BF)],
        compiler_params=_cparams(("parallel", "parallel")),
        name="attn_ctx",
    )(*ctx_args)


def _layer_norm(z, g, b):
    mu = jnp.mean(z, axis=-1, keepdims=True)
    zc = z - mu
    var = jnp.mean(zc * zc, axis=-1, keepdims=True)
    return zc * lax.rsqrt(var + NORM_EPS) * g + b


def _oproj_kernel(ot_ref, x_ref, wo_ref, g1_ref, lg_ref, lb_ref, sh2_ref, sc2_ref,
                  wrh_ref, wrl_ref, x1_ref, h2_ref, aff_ref):
    y = lax.dot_general(ot_ref[0], wo_ref[...], _TN, preferred_element_type=F32)
    x1 = _layer_norm(DEEPNORM_ALPHA * x_ref[0] + g1_ref[0] * y, lg_ref[...], lb_ref[...])
    x1_ref[0] = x1
    h2 = x1 * (1.0 + sc2_ref[0]) + sh2_ref[0]
    hb = h2.astype(BF)
    h2_ref[0] = hb
    hl = (h2 - hb.astype(F32)).astype(BF)
    lg = (lax.dot_general(wrh_ref[...], hb, _NT, preferred_element_type=F32)
          + lax.dot_general(wrl_ref[...], hb, _NT, preferred_element_type=F32)
          + lax.dot_general(wrh_ref[...], hl, _NT, preferred_element_type=F32))
    e = jnp.exp(lg - jnp.max(lg, axis=0, keepdims=True))
    aff_ref[0] = e / jnp.sum(e, axis=0, keepdims=True)


def _oproj_call(ot, x, mod, layer, nctx, rows_per_layer, w_o, ln_g, ln_b, w_router, g_lo, tm=512):
    g_n, n, d = x.shape
    n_e = w_router.shape[1]
    wr_t = w_router.T
    wrh = wr_t.astype(BF)
    wrl = (wr_t - wrh.astype(F32)).astype(BF)
    full = lambda shape: pl.BlockSpec(shape, lambda g, r: (0,) * len(shape))

    def mspec(chunk):
        def imap(g, r):
            gg = g + g_lo
            return (layer * rows_per_layer + jnp.where(gg < nctx, 0, gg - nctx + 1), 0, chunk)
        return pl.BlockSpec((1, 1, d), imap)

    return pl.pallas_call(
        _oproj_kernel,
        grid=(g_n - g_lo, n // tm),
        in_specs=[
            pl.BlockSpec((1, ot.shape[1], tm), lambda g, r: (g + g_lo, 0, r)),
            pl.BlockSpec((1, tm, d), lambda g, r: (g + g_lo, r, 0)),
            full(w_o.shape),
            mspec(2),
            full((1, d)), full((1, d)),
            mspec(3), mspec(4),
            full((n_e, d)), full((n_e, d)),
        ],
        out_specs=[
            pl.BlockSpec((1, tm, d), lambda g, r: (g, r, 0)),
            pl.BlockSpec((1, tm, d), lambda g, r: (g, r, 0)),
            pl.BlockSpec((1, n_e, tm), lambda g, r: (g, 0, r)),
        ],
        out_shape=[
            jax.ShapeDtypeStruct((g_n - g_lo, n, d), F32),
            jax.ShapeDtypeStruct((g_n - g_lo, n, d), BF),
            jax.ShapeDtypeStruct((g_n - g_lo, n_e, n), F32),
        ],
        compiler_params=_cparams(("parallel", "parallel")),
        name="oproj_norm_router",
    )(ot, x, w_o.astype(BF), mod, ln_g.reshape(1, d), ln_b.reshape(1, d), mod, mod, wrh, wrl)


def _route_segments(aff, seg_len, cap, tri):
    n_e, n = aff.shape
    nseg = n // seg_len
    bits = pltpu.bitcast(aff, jnp.int32)
    lane = lax.broadcasted_iota(jnp.int32, (n_e, n), 1)
    assert seg_len & (seg_len - 1) == 0
    pos = lane & (seg_len - 1)

    def seg_count(mask):
        v = jnp.where(mask, 1.0, 0.0)
        parts = []
        for s in range(nseg):
            c = jnp.sum(v[:, s * seg_len:(s + 1) * seg_len], axis=1, keepdims=True)
            parts.append(jnp.broadcast_to(c, (n_e, seg_len)))
        return parts[0] if nseg == 1 else jnp.concatenate(parts, axis=1)

    def thr_step(i, thr):
        cand = thr | (jnp.int32(1) << (30 - i))
        return jnp.where(seg_count(bits >= cand) >= cap, cand, thr)

    thr = lax.fori_loop(0, 31, thr_step, jnp.zeros((n_e, n), jnp.int32))
    gt = bits > thr
    eq = bits == thr
    need = cap - seg_count(gt)

    def tie_step(i, j):
        cand = j | (jnp.int32(1) << (seg_len.bit_length() - 2 - i))
        return jnp.where(seg_count(eq & (pos < cand)) < need, cand, j)

    j = lax.fori_loop(0, seg_len.bit_length() - 1, tie_step, jnp.zeros((n_e, n), jnp.int32))
    sel = gt | (eq & (pos <= j))

    selb = jnp.where(sel, 1.0, 0.0).astype(BF)
    tiles_per_seg = seg_len // LANE
    out = []
    run = jnp.zeros((n_e, 1), F32)
    for t in range(n // LANE):
        if t % tiles_per_seg == 0:
            run = jnp.full((n_e, 1), float((t // tiles_per_seg) * cap), F32)
        inc = jnp.dot(selb[:, t * LANE:(t + 1) * LANE], tri, preferred_element_type=F32)
        out.append(inc + (run - 1.0))
        run = run + inc[:, LANE - 1:LANE]
    slot = jnp.concatenate(out, axis=1).astype(jnp.int32)
    return jnp.where(sel, slot, -1)


def _route_kernel(aff_ref, tri_ref, slot_ref, *, nctx, n_ctx_len, g_lo):
    n = aff_ref.shape[2]
    g = pl.program_id(0) + g_lo

    if g_lo < nctx:
        @pl.when(g < nctx)
        def _():
            slot_ref[0] = _route_segments(aff_ref[0], n_ctx_len,
                                          CAPACITY_FACTOR * n_ctx_len // N_EXPERTS, tri_ref[...])

    @pl.when(g >= nctx)
    def _():
        slot_ref[0] = _route_segments(aff_ref[0], n, CAPACITY_FACTOR * n // N_EXPERTS, tri_ref[...])


def _route_call(aff, nctx, n_ctx_len, g_lo):
    g_n, n_e, n = aff.shape
    tri = jnp.triu(jnp.ones((LANE, LANE), F32)).astype(BF)
    return pl.pallas_call(
        functools.partial(_route_kernel, nctx=nctx, n_ctx_len=n_ctx_len, g_lo=g_lo),
        grid=(g_n,),
        in_specs=[pl.BlockSpec((1, n_e, n), lambda g: (g, 0, 0)),
                  pl.BlockSpec((LANE, LANE), lambda g: (0, 0))],
        out_specs=pl.BlockSpec((1, n_e, n), lambda g: (g, 0, 0)),
        out_shape=jax.ShapeDtypeStruct((g_n, n_e, n), jnp.int32),
        compiler_params=_cparams(("parallel",)),
        name="ec_route",
    )(aff, tri)


def _moe_kernel(h_ref, slot_ref, gate_ref, wg_ref, wu_ref, wd_ref, f_ref, *, n_slots, chunk):
    e = pl.program_id(1)
    n = h_ref.shape[1]

    @pl.when(e == 0)
    def _():
        f_ref[...] = jnp.zeros_like(f_ref)

    slot = slot_ref[0]
    hit = lax.broadcasted_iota(jnp.int32, (n_slots, n), 0) == slot
    sel = jnp.where(hit, 1.0, 0.0).astype(BF)
    selg = jnp.where(hit, gate_ref[0], 0.0).astype(BF)
    xs = jnp.dot(sel, h_ref[0], preferred_element_type=F32).astype(BF)
    a = jnp.dot(xs, wg_ref[0], preferred_element_type=F32)
    u = jnp.dot(xs, wu_ref[0], preferred_element_type=F32)
    hh = (a * jax.nn.sigmoid(a) * u).astype(BF)
    y = jnp.dot(hh, wd_ref[0], preferred_element_type=F32).astype(BF)
    for c in range(n // chunk):
        rows = slice(c * chunk, (c + 1) * chunk)
        f_ref[0, rows, :] += lax.dot_general(selg[:, rows], y, _TN, preferred_element_type=F32)


def _moe_call(h2, slot, aff, w_gate, w_up, w_down, n_slots):
    g_n, n, d = h2.shape
    n_e, _, ff = w_gate.shape
    slot3 = slot.reshape(g_n * n_e, 1, n)
    gate3 = aff.reshape(g_n * n_e, 1, n)
    return pl.pallas_call(
        functools.partial(_moe_kernel, n_slots=n_slots, chunk=256),
        grid=(g_n, n_e),
        in_specs=[
            pl.BlockSpec((1, n, d), lambda g, e: (g, 0, 0)),
            pl.BlockSpec((1, 1, n), lambda g, e: (g * n_e + e, 0, 0)),
            pl.BlockSpec((1, 1, n), lambda g, e: (g * n_e + e, 0, 0)),
            pl.BlockSpec((1, d, ff), lambda g, e: (e, 0, 0)),
            pl.BlockSpec((1, d, ff), lambda g, e: (e, 0, 0)),
            pl.BlockSpec((1, ff, d), lambda g, e: (e, 0, 0)),
        ],
        out_specs=pl.BlockSpec((1, n, d), lambda g, e: (g, 0, 0)),
        out_shape=jax.ShapeDtypeStruct((g_n, n, d), F32),
        compiler_params=_cparams(("parallel", "arbitrary")),
        name="ec_moe",
    )(h2, slot3, gate3, w_gate, w_up, w_down)


def _ffn_norm_kernel(x1_ref, f_ref, g2_ref, lg_ref, lb_ref, x2_ref):
    x2_ref[0] = _layer_norm(DEEPNORM_ALPHA * x1_ref[0] + g2_ref[0] * f_ref[0],
                            lg_ref[...], lb_ref[...])


def _ffn_norm_call(x1, f, mod, layer, nctx, rows_per_layer, ln_g, ln_b, g_lo, tm=512):
    g_n, n, d = x1.shape

    def imap(g, r):
        gg = g + g_lo
        return (layer * rows_per_layer + jnp.where(gg < nctx, 0, gg - nctx + 1), 0, 5)

    full = lambda shape: pl.BlockSpec(shape, lambda g, r: (0,) * len(shape))
    return pl.pallas_call(
        _ffn_norm_kernel,
        grid=(g_n, n // tm),
        in_specs=[
            pl.BlockSpec((1, tm, d), lambda g, r: (g, r, 0)),
            pl.BlockSpec((1, tm, d), lambda g, r: (g, r, 0)),
            pl.BlockSpec((1, 1, d), imap),
            full((1, d)), full((1, d)),
        ],
        out_specs=pl.BlockSpec((1, tm, d), lambda g, r: (g, r, 0)),
        out_shape=jax.ShapeDtypeStruct((g_n, n, d), F32),
        compiler_params=_cparams(("parallel", "parallel")),
        name="ffn_norm",
    )(x1, f, mod, ln_g.reshape(1, d), ln_b.reshape(1, d))


def kernel(x, c, ctx, c_ctx, ada_w, ada_b, ln_mix_g, ln_mix_b, ln_ffn_g, ln_ffn_b, router_w, expert_w_gate, expert_w_up, expert_w_down, gqa_w_qkv, gqa_q_g, gqa_k_g, gqa_w_o, mla_w_dq, mla_q_g, mla_w_uq, mla_w_dkv, mla_kv_g, mla_w_ukv, mla_w_o):
    batch, n, d = x.shape
    n_ctx_len = ctx.shape[1]
    depth = ada_w.shape[0]
    assert n % n_ctx_len == 0 and (batch * n_ctx_len) % n == 0
    nctx = batch * n_ctx_len // n
    n_slots = CAPACITY_FACTOR * n // N_EXPERTS

    rows = -(-(batch + 1) // MOD_ROWS_PAD) * MOD_ROWS_PAD
    cvec = jnp.concatenate([c_ctx[None, :], c, jnp.zeros((rows - batch - 1, d), F32)], axis=0)
    mod = _ada_call(cvec, ada_w, ada_b).reshape(depth * rows, 1, 6 * d)

    xs = jnp.concatenate([ctx.reshape(nctx, n, d), x], axis=0)
    for i in range(depth):
        last = i == depth - 1
        j = i // N_MIXERS
        if i % N_MIXERS == 0:
            qt, k, vt = _gqa_proj_call(xs, mod, i, nctx, rows, gqa_w_qkv[j], gqa_q_g[j], gqa_k_g[j])
            ot = _attn_call(qt, k, None, vt, nctx, n_ctx_len, GQA_HEADS, GQA_KV_HEADS, not last)
            w_o = gqa_w_o[j]
        else:
            qt, kn, kr, vt = _mla_proj_call(xs, mod, i, nctx, rows, mla_w_dq[j], mla_q_g[j], mla_w_uq[j],
                                            mla_w_dkv[j], mla_kv_g[j], mla_w_ukv[j])
            ot = _attn_call(qt, kn, kr, vt, nctx, n_ctx_len, MLA_HEADS, MLA_HEADS, not last)
            w_o = mla_w_o[j]
        g_lo = nctx if last else 0
        x1, h2, aff = _oproj_call(ot, xs, mod, i, nctx, rows, w_o, ln_mix_g[i], ln_mix_b[i],
                                  router_w[i], g_lo)
        slot = _route_call(aff, nctx, n_ctx_len, g_lo)
        f = _moe_call(h2, slot, aff, expert_w_gate[i].astype(BF), expert_w_up[i].astype(BF),
                      expert_w_down[i].astype(BF), n_slots)
        xs = _ffn_norm_call(x1, f, mod, i, nctx, rows, ln_ffn_g[i], ln_ffn_b[i], g_lo)
    return xs
```

```python
import functools

import jax
import jax.numpy as jnp
import numpy as np
from jax import lax
from jax.experimental import pallas as pl
from jax.experimental.pallas import tpu as pltpu

D_MODEL = 1024
DEPTH = 4
GRID_W = 64
ROPE_THETA = 10000.0
NORM_EPS = 1e-6
GQA_HEADS = 8
GQA_KV_HEADS = 2
GQA_HEAD_DIM = 128
MLA_HEADS = 8
MLA_Q_RANK = 768
MLA_KV_RANK = 256
MLA_NOPE_DIM = 128
MLA_ROPE_DIM = 64
MLA_V_DIM = 128
N_EXPERTS = 16
EXPERT_FF = 1024
CAPACITY_FACTOR = 2
N_MIXERS = 2
DEEPNORM_ALPHA = (2 * DEPTH) ** 0.25
LOG2E = 1.4426950408889634

LANE = 128
MOD_ROWS_PAD = 8
ROUTE_GROUPS_PER_STEP = 4
MOE_TOKEN_CHUNK = 256
MOE_SLOT_BLOCK = 128
VMEM_LIMIT = 56 * 1024 * 1024

BF = jnp.bfloat16
F32 = jnp.float32

_NT = (((1,), (1,)), ((), ()))
_TN = (((0,), (0,)), ((), ()))


def _cparams(sem):
    return pltpu.CompilerParams(dimension_semantics=sem, vmem_limit_bytes=VMEM_LIMIT)


def _ada_kernel(c_ref, w_ref, b_ref, o_ref):
    c = c_ref[...]
    s = (c * jax.nn.sigmoid(c)).astype(BF)
    o_ref[0] = jnp.dot(s, w_ref[0].astype(BF), preferred_element_type=F32) + b_ref[0]


def _ada_call(cvec, ada_w, ada_b):
    n_layers, d, n_out = ada_w.shape
    rows = cvec.shape[0]
    tn = 1536
    return pl.pallas_call(
        _ada_kernel,
        grid=(n_layers, n_out // tn),
        in_specs=[
            pl.BlockSpec((rows, d), lambda i, j: (0, 0)),
            pl.BlockSpec((1, d, tn), lambda i, j: (i, 0, j)),
            pl.BlockSpec((1, 1, tn), lambda i, j: (i, 0, j)),
        ],
        out_specs=pl.BlockSpec((1, rows, tn), lambda i, j: (i, 0, j)),
        out_shape=jax.ShapeDtypeStruct((n_layers, rows, n_out), F32),
        compiler_params=_cparams(("parallel", "parallel")),
        name="ada_mod",
    )(cvec, ada_w, ada_b.reshape(n_layers, 1, n_out))


def _rope_tables(n, rot_dim):
    t = jnp.arange(n, dtype=jnp.int32)
    row = (t // GRID_W).astype(F32)
    col = (t % GRID_W).astype(F32)
    axis_dim = rot_dim // 2
    freqs = ROPE_THETA ** (-jnp.arange(0, axis_dim, 2, dtype=F32) / axis_dim)
    ang = jnp.concatenate([row[:, None] * freqs, col[:, None] * freqs], axis=-1)
    return jnp.cos(ang), jnp.sin(ang)


def _gqa_proj_kernel(x_ref, sh_ref, sc_ref, wqv_ref, wk_ref, gq_ref, gk_ref,
                     cs_ref, sn_ref, cst_ref, snt_ref,
                     qt_ref, k_ref, vt_ref, *, nctx):
    is_ctx = pl.program_id(0) < nctx
    h = (x_ref[0] * (1.0 + sc_ref[0]) + sh_ref[0]).astype(BF)
    qv = lax.dot_general(wqv_ref[...], h, _NT, preferred_element_type=F32)
    cost = jnp.where(is_ctx, 1.0, cst_ref[...])
    sint = jnp.where(is_ctx, 0.0, snt_ref[...])
    hd, half = GQA_HEAD_DIM, GQA_HEAD_DIM // 2
    for i in range(GQA_HEADS):
        q = qv[i * hd:(i + 1) * hd]
        ms = jnp.mean(q * q, axis=0, keepdims=True)
        qn = q * lax.rsqrt(ms + NORM_EPS) * gq_ref[...]
        x0, x1 = qn[:half], qn[half:]
        qt_ref[0, i * hd:i * hd + half, :] = (x0 * cost - x1 * sint).astype(BF)
        qt_ref[0, i * hd + half:(i + 1) * hd, :] = (x0 * sint + x1 * cost).astype(BF)
    vt_ref[0] = qv[GQA_HEADS * hd:].astype(BF)
    kk = jnp.dot(h, wk_ref[...], preferred_element_type=F32)
    cos = jnp.where(is_ctx, 1.0, cs_ref[...])
    sin = jnp.where(is_ctx, 0.0, sn_ref[...])
    for j in range(GQA_KV_HEADS):
        k = kk[:, j * hd:(j + 1) * hd]
        ms = jnp.mean(k * k, axis=1, keepdims=True)
        kn = k * lax.rsqrt(ms + NORM_EPS) * gk_ref[...]
        k_ref[0, :, j * hd:(j + 1) * hd] = (kn * cos + pltpu.roll(kn, half, 1) * sin).astype(BF)


def _mod_spec(layer, chunk, nctx, rows_per_layer, d):
    def imap(g, *_):
        return (layer * rows_per_layer + jnp.where(g < nctx, 0, g - nctx + 1), 0, chunk)
    return pl.BlockSpec((1, 1, d), imap)


def _gqa_proj_call(x, mod, layer, nctx, rows_per_layer, w_qkv, q_g, k_g, tm=512):
    g_n, n, d = x.shape
    hd, half = GQA_HEAD_DIM, GQA_HEAD_DIM // 2
    nq, nk = GQA_HEADS * hd, GQA_KV_HEADS * hd
    perm = np.concatenate([np.arange(0, hd, 2), np.arange(1, hd, 2)])
    wq = w_qkv[:, :nq].reshape(d, GQA_HEADS, hd)[:, :, perm].reshape(d, nq)
    wk = w_qkv[:, nq:nq + nk].reshape(d, GQA_KV_HEADS, hd)[:, :, perm].reshape(d, nk)
    wv = w_qkv[:, nq + nk:]
    wqv_t = jnp.concatenate([wq, wv], axis=1).T.astype(BF)
    wk = wk.astype(BF)
    gq = (q_g[perm] * (hd ** -0.5 * LOG2E)).reshape(hd, 1)
    gk = k_g[perm].reshape(1, hd)
    cos, sin = _rope_tables(n, hd)
    cs = jnp.concatenate([cos, cos], axis=1)
    sn = jnp.concatenate([-sin, sin], axis=1)
    kernel = functools.partial(_gqa_proj_kernel, nctx=nctx)
    full = lambda shape: pl.BlockSpec(shape, lambda g, r: (0,) * len(shape))
    return pl.pallas_call(
        kernel,
        grid=(g_n, n // tm),
        in_specs=[
            pl.BlockSpec((1, tm, d), lambda g, r: (g, r, 0)),
            _mod_spec(layer, 0, nctx, rows_per_layer, d),
            _mod_spec(layer, 1, nctx, rows_per_layer, d),
            full((nq + nk, d)),
            full((d, nk)),
            full((hd, 1)),
            full((1, hd)),
            pl.BlockSpec((tm, hd), lambda g, r: (r, 0)),
            pl.BlockSpec((tm, hd), lambda g, r: (r, 0)),
            pl.BlockSpec((half, tm), lambda g, r: (0, r)),
            pl.BlockSpec((half, tm), lambda g, r: (0, r)),
        ],
        out_specs=[
            pl.BlockSpec((1, nq, tm), lambda g, r: (g, 0, r)),
            pl.BlockSpec((1, tm, nk), lambda g, r: (g, r, 0)),
            pl.BlockSpec((1, nk, tm), lambda g, r: (g, 0, r)),
        ],
        out_shape=[
            jax.ShapeDtypeStruct((g_n, nq, n), BF),
            jax.ShapeDtypeStruct((g_n, n, nk), BF),
            jax.ShapeDtypeStruct((g_n, nk, n), BF),
        ],
        compiler_params=_cparams(("parallel", "parallel")),
        name="gqa_proj",
    )(x, mod, mod, wqv_t, wk, gq, gk, cs, sn, cos.T, sin.T)


MLA_QK_PAD = 256


def _mla_proj_kernel(x_ref, sh_ref, sc_ref, wdq_ref, gq_ref, wuq_ref, wdkv_ref, gkv_ref,
                     wukn_ref, wuv_ref, cs_ref, sn_ref, cst_ref, snt_ref,
                     qt_ref, kn_ref, kr_ref, vt_ref, *, nctx):
    is_ctx = pl.program_id(0) < nctx
    h = (x_ref[0] * (1.0 + sc_ref[0]) + sh_ref[0]).astype(BF)
    cq = jnp.dot(h, wdq_ref[...], preferred_element_type=F32)
    ms = jnp.mean(cq * cq, axis=1, keepdims=True)
    cqn = (cq * lax.rsqrt(ms + NORM_EPS) * gq_ref[...]).astype(BF)
    qt = lax.dot_general(wuq_ref[...], cqn, _NT, preferred_element_type=F32)
    qt = qt * ((MLA_NOPE_DIM + MLA_ROPE_DIM) ** -0.5 * LOG2E)
    cost = jnp.where(is_ctx, 1.0, cst_ref[...])
    sint = jnp.where(is_ctx, 0.0, snt_ref[...])
    hr = MLA_ROPE_DIM // 2
    for i in range(MLA_HEADS):
        base = i * MLA_QK_PAD
        qt_ref[0, base:base + MLA_NOPE_DIM, :] = qt[base:base + MLA_NOPE_DIM].astype(BF)
        r0 = base + MLA_NOPE_DIM
        x0 = qt[r0:r0 + hr]
        x1 = qt[r0 + 2 * hr:r0 + 3 * hr]
        zero = jnp.zeros_like(x0).astype(BF)
        qt_ref[0, r0:r0 + hr, :] = (x0 * cost - x1 * sint).astype(BF)
        qt_ref[0, r0 + hr:r0 + 2 * hr, :] = zero
        qt_ref[0, r0 + 2 * hr:r0 + 3 * hr, :] = (x0 * sint + x1 * cost).astype(BF)
        qt_ref[0, r0 + 3 * hr:r0 + 4 * hr, :] = zero
    ckv = jnp.dot(h, wdkv_ref[...], preferred_element_type=F32)
    c = ckv[:, :MLA_KV_RANK]
    ms = jnp.mean(c * c, axis=1, keepdims=True)
    cn = (c * lax.rsqrt(ms + NORM_EPS) * gkv_ref[...]).astype(BF)
    kr = ckv[:, MLA_KV_RANK:]
    cos = jnp.where(is_ctx, 1.0, cs_ref[...])
    sin = jnp.where(is_ctx, 0.0, sn_ref[...])
    kr_ref[0] = (kr * cos + pltpu.roll(kr, LANE // 2, 1) * sin).astype(BF)
    kn_ref[0] = jnp.dot(cn, wukn_ref[...], preferred_element_type=F32).astype(BF)
    vt_ref[0] = lax.dot_general(wuv_ref[...], cn, _NT, preferred_element_type=F32).astype(BF)


def _mla_proj_call(x, mod, layer, nctx, rows_per_layer, w_dq, q_g, w_uq, w_dkv, kv_g, w_ukv, tm=512):
    g_n, n, d = x.shape
    nh, dn, dr, dv = MLA_HEADS, MLA_NOPE_DIM, MLA_ROPE_DIM, MLA_V_DIM
    hr = dr // 2
    ev, od = np.arange(0, dr, 2), np.arange(1, dr, 2)
    wu = w_uq.reshape(MLA_Q_RANK, nh, dn + dr)
    zq = jnp.zeros((MLA_Q_RANK, nh, hr), F32)
    wu = jnp.concatenate([wu[:, :, :dn], wu[:, :, dn + ev], zq, wu[:, :, dn + od], zq], axis=2)
    wuq_t = wu.reshape(MLA_Q_RANK, nh * MLA_QK_PAD).T.astype(BF)
    zk = jnp.zeros((d, hr), F32)
    wdkv = jnp.concatenate([w_dkv[:, :MLA_KV_RANK], w_dkv[:, MLA_KV_RANK + ev], zk,
                            w_dkv[:, MLA_KV_RANK + od], zk], axis=1).astype(BF)
    wkv = w_ukv.reshape(MLA_KV_RANK, nh, dn + dv)
    wukn = wkv[:, :, :dn].reshape(MLA_KV_RANK, nh * dn).astype(BF)
    wuv_t = wkv[:, :, dn:].reshape(MLA_KV_RANK, nh * dv).T.astype(BF)
    cos, sin = _rope_tables(n, dr)
    z = jnp.zeros_like(cos)
    cs = jnp.concatenate([cos, z, cos, z], axis=1)
    sn = jnp.concatenate([-sin, z, sin, z], axis=1)
    kernel = functools.partial(_mla_proj_kernel, nctx=nctx)
    full = lambda shape: pl.BlockSpec(shape, lambda g, r: (0,) * len(shape))
    return pl.pallas_call(
        kernel,
        grid=(g_n, n // tm),
        in_specs=[
            pl.BlockSpec((1, tm, d), lambda g, r: (g, r, 0)),
            _mod_spec(layer, 0, nctx, rows_per_layer, d),
            _mod_spec(layer, 1, nctx, rows_per_layer, d),
            full((d, MLA_Q_RANK)),
            full((1, MLA_Q_RANK)),
            full((nh * MLA_QK_PAD, MLA_Q_RANK)),
            full((d, MLA_KV_RANK + LANE)),
            full((1, MLA_KV_RANK)),
            full((MLA_KV_RANK, nh * dn)),
            full((nh * dv, MLA_KV_RANK)),
            pl.BlockSpec((tm, LANE), lambda g, r: (r, 0)),
            pl.BlockSpec((tm, LANE), lambda g, r: (r, 0)),
            pl.BlockSpec((hr, tm), lambda g, r: (0, r)),
            pl.BlockSpec((hr, tm), lambda g, r: (0, r)),
        ],
        out_specs=[
            pl.BlockSpec((1, nh * MLA_QK_PAD, tm), lambda g, r: (g, 0, r)),
            pl.BlockSpec((1, tm, nh * dn), lambda g, r: (g, r, 0)),
            pl.BlockSpec((1, tm, LANE), lambda g, r: (g, r, 0)),
            pl.BlockSpec((1, nh * dv, tm), lambda g, r: (g, 0, r)),
        ],
        out_shape=[
            jax.ShapeDtypeStruct((g_n, nh * MLA_QK_PAD, n), BF),
            jax.ShapeDtypeStruct((g_n, n, nh * dn), BF),
            jax.ShapeDtypeStruct((g_n, n, LANE), BF),
            jax.ShapeDtypeStruct((g_n, nh * dv, n), BF),
        ],
        compiler_params=_cparams(("parallel", "parallel")),
        name="mla_proj",
    )(x, mod, mod, w_dq.astype(BF), q_g.reshape(1, -1), wuq_t, wdkv, kv_g.reshape(1, -1),
      wukn, wuv_t, cs, sn, cos.T, sin.T)


SUBLANE = 8


def _attn_kernel(*refs, mla, with_lat, hb, rep, dq, dv, ck, sb):
    refs = list(refs)
    qt_ref = refs.pop(0)
    kl_ref = refs.pop(0) if with_lat else None
    kc_ref = refs.pop(0)
    krl_ref = refs.pop(0) if (mla and with_lat) else None
    krc_ref = refs.pop(0) if mla else None
    vtl_ref = refs.pop(0) if with_lat else None
    vtc_ref = refs.pop(0)
    p_scr = refs.pop()
    s_scr = refs.pop()
    ot_ref = refs.pop()

    def pipeline(q0, tq, chunks):
        def stage_a(u, chunk):
            k_ref, kr_ref, _, r0, rows, s0 = chunk
            j = u // rep
            keys = k_ref[0, r0:r0 + rows, j * LANE:(j + 1) * LANE]
            if mla:
                keys = jnp.concatenate([keys, kr_ref[0, r0:r0 + rows, :]], axis=1)
            s = jnp.dot(keys, qt_ref[0, u * dq:(u + 1) * dq, q0:q0 + tq],
                        preferred_element_type=F32)
            s_scr[u % 2, s0:s0 + rows, :] = s
            return jnp.max(s.reshape(rows // SUBLANE, SUBLANE, tq), axis=0)

        def stage_b(u, ci, chunk, m):
            _, _, vt_ref, r0, rows, s0 = chunk
            j = u // rep
            l8 = None
            for r in range(0, rows, sb):
                p = jnp.exp2(s_scr[u % 2, s0 + r:s0 + r + sb, :] - m)
                ps = jnp.sum(p.reshape(sb // SUBLANE, SUBLANE, tq), axis=0)
                l8 = ps if l8 is None else l8 + ps
                p_scr[ci % 2, r:r + sb, :] = p.astype(BF)
            pv = jnp.dot(vt_ref[0, j * dv:(j + 1) * dv, r0:r0 + rows], p_scr[ci % 2, :rows, :],
                         preferred_element_type=F32)
            return l8, pv

        m_cur = None
        for u in range(hb + 1):
            m_parts, l_parts, acc = [], [], None
            for ci, chunk in enumerate(chunks):
                if u < hb:
                    m_parts.append(stage_a(u, chunk))
                if u >= 1:
                    l8, pv = stage_b(u - 1, ci, chunk, m_cur)
                    l_parts.append(l8)
                    acc = pv if acc is None else acc + pv
            if u >= 1:
                l = jnp.sum(functools.reduce(jnp.add, l_parts), axis=0, keepdims=True)
                ot_ref[0, (u - 1) * dv:u * dv, q0:q0 + tq] = (acc / l).astype(BF)
            if u < hb:
                m_cur = jnp.max(functools.reduce(jnp.maximum, m_parts), axis=0, keepdims=True)

    if with_lat:
        n_ctx = kc_ref.shape[1]
        chunks = [(kc_ref, krc_ref, vtc_ref, 0, n_ctx, 0)]
        for r0 in range(0, kl_ref.shape[1], ck):
            chunks.append((kl_ref, krl_ref, vtl_ref, r0, ck, n_ctx + r0))
        pipeline(0, qt_ref.shape[2], chunks)
    else:
        n_ctx = s_scr.shape[1]
        for s in range(kc_ref.shape[1] // n_ctx):
            pipeline(s * n_ctx, n_ctx, [(kc_ref, krc_ref, vtc_ref, s * n_ctx, n_ctx, 0)])


def _attn_call(qt, k, kr, vt, nctx, n_ctx_len, n_heads, n_kv_heads, with_ctx_out, tq=512, ck=256, sb=32):
    g_n, _, n = qt.shape
    dq = qt.shape[1] // n_heads
    dv = vt.shape[1] // n_kv_heads
    batch = g_n - nctx
    spg = n // n_ctx_len
    rep = n_heads // n_kv_heads
    hb = n_heads
    hkv = hb // rep
    mla = kr is not None
    ot_shape = jax.ShapeDtypeStruct((g_n, n_heads * dv, n), BF)
    kern = functools.partial(_attn_kernel, mla=mla, hb=hb, rep=rep, dq=dq, dv=dv, ck=ck, sb=sb)

    lat_in = [pl.BlockSpec((1, hb * dq, tq), lambda b, h, t: (nctx + b, h, t)),
              pl.BlockSpec((1, n, hkv * LANE), lambda b, h, t: (nctx + b, 0, h)),
              pl.BlockSpec((1, n_ctx_len, hkv * LANE), lambda b, h, t: (b // spg, b % spg, h))]
    lat_args = [qt, k, k]
    if mla:
        lat_in += [pl.BlockSpec((1, n, LANE), lambda b, h, t: (nctx + b, 0, 0)),
                   pl.BlockSpec((1, n_ctx_len, LANE), lambda b, h, t: (b // spg, b % spg, 0))]
        lat_args += [kr, kr]
    lat_in += [pl.BlockSpec((1, hkv * dv, n), lambda b, h, t: (nctx + b, h, 0)),
               pl.BlockSpec((1, hkv * dv, n_ctx_len), lambda b, h, t: (b // spg, h, b % spg))]
    lat_args += [vt, vt]
    ot = pl.pallas_call(
        functools.partial(kern, with_lat=True),
        grid=(batch, n_heads // hb, n // tq),
        in_specs=lat_in,
        out_specs=pl.BlockSpec((1, hb * dv, tq), lambda b, h, t: (nctx + b, h, t)),
        out_shape=ot_shape,
        scratch_shapes=[pltpu.VMEM((2, n_ctx_len + n, tq), F32),
                        pltpu.VMEM((2, max(ck, n_ctx_len), tq), BF)],
        compiler_params=_cparams(("parallel", "parallel", "parallel")),
        name="attn_latent",
    )(*lat_args)
    if not with_ctx_out:
        return ot

    ctx_in = [pl.BlockSpec((1, hb * dq, n), lambda g: (g, 0, 0)),
              pl.BlockSpec((1, n, hkv * LANE), lambda g: (g, 0, 0))]
    ctx_args = [qt, k]
    if mla:
        ctx_in += [pl.BlockSpec((1, n, LANE), lambda g: (g, 0, 0))]
        ctx_args += [kr]
    ctx_in += [pl.BlockSpec((1, hkv * dv, n), lambda g: (g, 0, 0)),
               pl.BlockSpec(memory_space=pl.ANY)]
    ctx_args += [vt, ot]
    return pl.pallas_call(
        functools.partial(kern, with_lat=False),
        grid=(nctx,),
        in_specs=ctx_in,
        out_specs=pl.BlockSpec((1, hb * dv, n), lambda g: (g, 0, 0)),
        out_shape=ot_shape,
        input_output_aliases={len(ctx_args) - 1: 0},
        scratch_shapes=[pltpu.VMEM((2, n_ctx_len, n_ctx_len), F32),
                        pltpu.VMEM((2, n_ctx_len, n_ctx_len), BF)],
        compiler_params=_cparams(("parallel",)),
        name="attn_ctx",
    )(*ctx_args)


def _layer_norm(z, g, b):
    mu = jnp.mean(z, axis=-1, keepdims=True)
    zc = z - mu
    var = jnp.mean(zc * zc, axis=-1, keepdims=True)
    return zc * lax.rsqrt(var + NORM_EPS) * g + b


def _oproj_kernel(ot_ref, x_ref, wo_ref, g1_ref, lg_ref, lb_ref, sh2_ref, sc2_ref,
                  wrh_ref, wrl_ref, x1_ref, h2_ref, aff_ref):
    y = lax.dot_general(ot_ref[0], wo_ref[...], _TN, preferred_element_type=F32)
    x1 = _layer_norm(DEEPNORM_ALPHA * x_ref[0] + g1_ref[0] * y, lg_ref[...], lb_ref[...])
    x1_ref[0] = x1
    h2 = x1 * (1.0 + sc2_ref[0]) + sh2_ref[0]
    hb = h2.astype(BF)
    h2_ref[0] = hb
    hl = (h2 - hb.astype(F32)).astype(BF)
    lg = (lax.dot_general(wrh_ref[...], hb, _NT, preferred_element_type=F32)
          + lax.dot_general(wrl_ref[...], hb, _NT, preferred_element_type=F32)
          + lax.dot_general(wrh_ref[...], hl, _NT, preferred_element_type=F32))
    e = jnp.exp(lg - jnp.max(lg, axis=0, keepdims=True))
    aff_ref[0] = e / jnp.sum(e, axis=0, keepdims=True)


def _oproj_call(ot, x, mod, layer, nctx, rows_per_layer, w_o, ln_g, ln_b, w_router, g_lo, tm=512):
    g_n, n, d = x.shape
    n_e = w_router.shape[1]
    wr_t = w_router.T
    wrh = wr_t.astype(BF)
    wrl = (wr_t - wrh.astype(F32)).astype(BF)
    full = lambda shape: pl.BlockSpec(shape, lambda g, r: (0,) * len(shape))

    def mspec(chunk):
        def imap(g, r):
            gg = g + g_lo
            return (layer * rows_per_layer + jnp.where(gg < nctx, 0, gg - nctx + 1), 0, chunk)
        return pl.BlockSpec((1, 1, d), imap)

    return pl.pallas_call(
        _oproj_kernel,
        grid=(g_n - g_lo, n // tm),
        in_specs=[
            pl.BlockSpec((1, ot.shape[1], tm), lambda g, r: (g + g_lo, 0, r)),
            pl.BlockSpec((1, tm, d), lambda g, r: (g + g_lo, r, 0)),
            full(w_o.shape),
            mspec(2),
            full((1, d)), full((1, d)),
            mspec(3), mspec(4),
            full((n_e, d)), full((n_e, d)),
        ],
        out_specs=[
            pl.BlockSpec((1, tm, d), lambda g, r: (g, r, 0)),
            pl.BlockSpec((1, tm, d), lambda g, r: (g, r, 0)),
            pl.BlockSpec((1, n_e, tm), lambda g, r: (g, 0, r)),
        ],
        out_shape=[
            jax.ShapeDtypeStruct((g_n - g_lo, n, d), F32),
            jax.ShapeDtypeStruct((g_n - g_lo, n, d), BF),
            jax.ShapeDtypeStruct((g_n - g_lo, n_e, n), F32),
        ],
        compiler_params=_cparams(("parallel", "parallel")),
        name="oproj_norm_router",
    )(ot, x, w_o.astype(BF), mod, ln_g.reshape(1, d), ln_b.reshape(1, d), mod, mod, wrh, wrl)


def _route_segments(aff, seg_len, cap, tri):
    n_e, n = aff.shape
    nseg = n // seg_len
    bits = pltpu.bitcast(aff, jnp.int32)
    lane = lax.broadcasted_iota(jnp.int32, (n_e, n), 1)
    assert seg_len & (seg_len - 1) == 0
    pos = lane & (seg_len - 1)

    def seg_count(mask):
        v = jnp.where(mask, 1.0, 0.0)
        parts = []
        for s in range(nseg):
            c = jnp.sum(v[:, s * seg_len:(s + 1) * seg_len], axis=1, keepdims=True)
            parts.append(jnp.broadcast_to(c, (n_e, seg_len)))
        return parts[0] if nseg == 1 else jnp.concatenate(parts, axis=1)

    def thr_step(i, thr):
        cand = thr | (jnp.int32(1) << (30 - i))
        return jnp.where(seg_count(bits >= cand) >= cap, cand, thr)

    thr = lax.fori_loop(0, 31, thr_step, jnp.zeros((n_e, n), jnp.int32))
    gt = bits > thr
    eq = bits == thr
    need = cap - seg_count(gt)

    def tie_step(i, j):
        cand = j | (jnp.int32(1) << (seg_len.bit_length() - 2 - i))
        return jnp.where(seg_count(eq & (pos < cand)) < need, cand, j)

    j = lax.fori_loop(0, seg_len.bit_length() - 1, tie_step, jnp.zeros((n_e, n), jnp.int32))
    sel = gt | (eq & (pos <= j))

    selb = jnp.where(sel, 1.0, 0.0).astype(BF)
    tiles_per_seg = seg_len // LANE
    tiles_per_chunk = MOE_TOKEN_CHUNK // LANE
    lane1 = lax.broadcasted_iota(jnp.int32, (n_e, LANE), 1)
    starts = jnp.zeros((n_e, LANE), F32)
    out = []
    run = jnp.zeros((n_e, 1), F32)
    for t in range(n // LANE):
        if t % tiles_per_seg == 0:
            run = jnp.full((n_e, 1), float((t // tiles_per_seg) * cap), F32)
        if t % tiles_per_chunk == 0:
            starts = jnp.where(lane1 == t // tiles_per_chunk, run, starts)
        inc = jnp.dot(selb[:, t * LANE:(t + 1) * LANE], tri, preferred_element_type=F32)
        out.append(inc + (run - 1.0))
        run = run + inc[:, LANE - 1:LANE]
    starts = jnp.where(lane1 == n // MOE_TOKEN_CHUNK, run, starts)
    slot = jnp.concatenate(out, axis=1).astype(jnp.int32)
    return jnp.where(sel, slot, -1), starts.astype(jnp.int32)


def _route_kernel(aff_ref, tri_ref, slot_ref, starts_ref, *, nctx, n_ctx_len, g_lo):
    n = aff_ref.shape[2]
    gb, n_e = aff_ref.shape[0], aff_ref.shape[1]
    g = pl.program_id(0) * gb + g_lo

    def run(seg_len):
        slot, starts = _route_segments(aff_ref[...].reshape(gb * n_e, n), seg_len,
                                       CAPACITY_FACTOR * seg_len // N_EXPERTS, tri_ref[...])
        slot_ref[...] = slot.reshape(gb, n_e, n)
        starts_ref[...] = starts.reshape(gb, n_e, LANE)

    if g_lo < nctx:
        pl.when(g < nctx)(functools.partial(run, n_ctx_len))
    pl.when(g >= nctx)(functools.partial(run, n))


def _route_call(aff, nctx, n_ctx_len, g_lo):
    g_n, n_e, n = aff.shape
    assert n_ctx_len % MOE_TOKEN_CHUNK == 0 and n // MOE_TOKEN_CHUNK < LANE
    gb = ROUTE_GROUPS_PER_STEP
    if (nctx - g_lo) % gb or g_n % gb:
        gb = 1
    tri = jnp.triu(jnp.ones((LANE, LANE), F32)).astype(BF)
    return pl.pallas_call(
        functools.partial(_route_kernel, nctx=nctx, n_ctx_len=n_ctx_len, g_lo=g_lo),
        grid=(g_n // gb,),
        in_specs=[pl.BlockSpec((gb, n_e, n), lambda g: (g, 0, 0)),
                  pl.BlockSpec((LANE, LANE), lambda g: (0, 0))],
        out_specs=[pl.BlockSpec((gb, n_e, n), lambda g: (g, 0, 0)),
                   pl.BlockSpec((gb, n_e, LANE), lambda g: (g, 0, 0))],
        out_shape=[jax.ShapeDtypeStruct((g_n, n_e, n), jnp.int32),
                   jax.ShapeDtypeStruct((g_n, n_e, LANE), jnp.int32)],
        compiler_params=_cparams(("parallel",)),
        name="ec_route",
    )(aff, tri)


def _gather_pairs(starts, n_slots, n_chunks):
    n_blocks = n_slots // MOE_SLOT_BLOCK
    n_pairs = n_blocks + n_chunks - 1
    assert n_blocks <= 16 and n_chunks <= 16
    s = starts[..., :n_chunks + 1]
    lo = jnp.arange(n_blocks, dtype=jnp.int32) * MOE_SLOT_BLOCK
    c_lo = jnp.sum(s[..., None, :] <= lo[:, None], axis=-1) - 1
    c_hi = jnp.sum(s[..., None, :] <= lo[:, None] + (MOE_SLOT_BLOCK - 1), axis=-1) - 1
    cnt = c_hi - c_lo + 1
    first = jnp.cumsum(cnt, axis=-1) - cnt
    j = jnp.arange(n_pairs, dtype=jnp.int32)
    r = jnp.sum(first[..., None, :] <= j[:, None], axis=-1) - 1
    c = jnp.take_along_axis(c_lo, r, axis=-1) + j - jnp.take_along_axis(first, r, axis=-1)
    valid = j < jnp.sum(cnt, axis=-1, keepdims=True)
    return jnp.where(valid, 256 + r * 16 + c, 0).astype(jnp.int32).reshape(-1), n_pairs


def _moe_kernel(pairs_ref, h_ref, slot_ref, gate_ref, wg_ref, wu_ref, wd_ref,
                x1_ref, g2_ref, lg_ref, lb_ref, x2_ref, xs_scr, *, n_slots, n_pairs):
    g, e = pl.program_id(0), pl.program_id(1)
    n_e = pl.num_programs(1)
    n_chunks, chunk = slot_ref.shape[1], slot_ref.shape[2]
    sb = MOE_SLOT_BLOCK

    @pl.when(e == 0)
    def _():
        x2_ref[...] = jnp.zeros_like(x2_ref)

    xs_scr[...] = jnp.zeros_like(xs_scr)
    base = (g * n_e + e) * n_pairs
    row_iota = lax.broadcasted_iota(jnp.int32, (sb, chunk), 0)
    for j in range(n_pairs):
        code = pairs_ref[base + j]
        c = code & 15
        r = (code >> 4) & 15
        rel = jnp.where(code >= 256, slot_ref[0, pl.ds(c, 1), :] - r * sb, -1)
        onehot = jnp.where(row_iota == rel, 1.0, 0.0).astype(BF)
        tok0 = pl.multiple_of(c * chunk, chunk)
        s0 = pl.multiple_of(r * sb, sb)
        xs_scr[pl.ds(s0, sb), :] += jnp.dot(onehot, h_ref[0, pl.ds(tok0, chunk), :],
                                            preferred_element_type=F32)
    xs = xs_scr[...].astype(BF)

    a = jnp.dot(xs, wg_ref[0], preferred_element_type=F32)
    u = jnp.dot(xs, wu_ref[0], preferred_element_type=F32)
    hh = (a * jax.nn.sigmoid(a) * u).astype(BF)
    y = jnp.dot(hh, wd_ref[0], preferred_element_type=F32).astype(BF)

    slot_iota = lax.broadcasted_iota(jnp.int32, (n_slots, chunk), 0)
    for c in range(n_chunks):
        rows = slice(c * chunk, (c + 1) * chunk)
        selg = jnp.where(slot_iota == slot_ref[0, c:c + 1, :], gate_ref[0, c:c + 1, :], 0.0).astype(BF)
        x2_ref[0, rows, :] += lax.dot_general(selg, y, _TN, preferred_element_type=F32)

    @pl.when(e == n_e - 1)
    def _():
        for c in range(n_chunks):
            rows = slice(c * chunk, (c + 1) * chunk)
            x2_ref[0, rows, :] = _layer_norm(DEEPNORM_ALPHA * x1_ref[0, rows, :] + g2_ref[0] * x2_ref[0, rows, :],
                                             lg_ref[...], lb_ref[...])


def _moe_call(h2, slot, starts, aff, w_gate, w_up, w_down, n_slots,
              x1, mod, layer, nctx, rows_per_layer, ln_g, ln_b, g_lo):
    g_n, n, d = h2.shape
    n_e, _, ff = w_gate.shape
    n_chunks = n // MOE_TOKEN_CHUNK
    slot3 = slot.reshape(g_n * n_e, n_chunks, MOE_TOKEN_CHUNK)
    gate3 = aff.reshape(g_n * n_e, n_chunks, MOE_TOKEN_CHUNK)
    pairs, n_pairs = _gather_pairs(starts, n_slots, n_chunks)

    def mod_map(g, e, p):
        gg = g + g_lo
        return (layer * rows_per_layer + jnp.where(gg < nctx, 0, gg - nctx + 1), 0, 5)

    full = lambda shape: pl.BlockSpec(shape, lambda g, e, p: (0,) * len(shape))
    grid_spec = pltpu.PrefetchScalarGridSpec(
        num_scalar_prefetch=1,
        grid=(g_n, n_e),
        in_specs=[
            pl.BlockSpec((1, n, d), lambda g, e, p: (g, 0, 0)),
            pl.BlockSpec((1, n_chunks, MOE_TOKEN_CHUNK), lambda g, e, p: (g * n_e + e, 0, 0)),
            pl.BlockSpec((1, n_chunks, MOE_TOKEN_CHUNK), lambda g, e, p: (g * n_e + e, 0, 0)),
            pl.BlockSpec((1, d, ff), lambda g, e, p: (e, 0, 0)),
            pl.BlockSpec((1, d, ff), lambda g, e, p: (e, 0, 0)),
            pl.BlockSpec((1, ff, d), lambda g, e, p: (e, 0, 0)),
            pl.BlockSpec((1, n, d), lambda g, e, p: (g, 0, 0), pipeline_mode=pl.Buffered(1)),
            pl.BlockSpec((1, 1, d), mod_map),
            full((1, d)), full((1, d)),
        ],
        out_specs=pl.BlockSpec((1, n, d), lambda g, e, p: (g, 0, 0)),
        scratch_shapes=[pltpu.VMEM((n_slots, d), F32)],
    )
    return pl.pallas_call(
        functools.partial(_moe_kernel, n_slots=n_slots, n_pairs=n_pairs),
        grid_spec=grid_spec,
        out_shape=jax.ShapeDtypeStruct((g_n, n, d), F32),
        compiler_params=_cparams(("parallel", "arbitrary")),
        name="ec_moe",
    )(pairs, h2, slot3, gate3, w_gate, w_up, w_down, x1, mod, ln_g.reshape(1, d), ln_b.reshape(1, d))


def kernel(x, c, ctx, c_ctx, ada_w, ada_b, ln_mix_g, ln_mix_b, ln_ffn_g, ln_ffn_b, router_w, expert_w_gate, expert_w_up, expert_w_down, gqa_w_qkv, gqa_q_g, gqa_k_g, gqa_w_o, mla_w_dq, mla_q_g, mla_w_uq, mla_w_dkv, mla_kv_g, mla_w_ukv, mla_w_o):
    batch, n, d = x.shape
    n_ctx_len = ctx.shape[1]
    depth = ada_w.shape[0]
    assert n % n_ctx_len == 0 and (batch * n_ctx_len) % n == 0
    nctx = batch * n_ctx_len // n
    n_slots = CAPACITY_FACTOR * n // N_EXPERTS

    rows = -(-(batch + 1) // MOD_ROWS_PAD) * MOD_ROWS_PAD
    cvec = jnp.concatenate([c_ctx[None, :], c, jnp.zeros((rows - batch - 1, d), F32)], axis=0)
    mod = _ada_call(cvec, ada_w, ada_b).reshape(depth * rows, 1, 6 * d)

    xs = jnp.concatenate([ctx.reshape(nctx, n, d), x], axis=0)
    for i in range(depth):
        last = i == depth - 1
        j = i // N_MIXERS
        if i % N_MIXERS == 0:
            qt, k, vt = _gqa_proj_call(xs, mod, i, nctx, rows, gqa_w_qkv[j], gqa_q_g[j], gqa_k_g[j])
            ot = _attn_call(qt, k, None, vt, nctx, n_ctx_len, GQA_HEADS, GQA_KV_HEADS, not last)
            w_o = gqa_w_o[j]
        else:
            qt, kn, kr, vt = _mla_proj_call(xs, mod, i, nctx, rows, mla_w_dq[j], mla_q_g[j], mla_w_uq[j],
                                            mla_w_dkv[j], mla_kv_g[j], mla_w_ukv[j])
            ot = _attn_call(qt, kn, kr, vt, nctx, n_ctx_len, MLA_HEADS, MLA_HEADS, not last)
            w_o = mla_w_o[j]
        g_lo = nctx if last else 0
        x1, h2, aff = _oproj_call(ot, xs, mod, i, nctx, rows, w_o, ln_mix_g[i], ln_mix_b[i],
                                  router_w[i], g_lo)
        slot, starts = _route_call(aff, nctx, n_ctx_len, g_lo)
        xs = _moe_call(h2, slot, starts, aff, expert_w_gate[i].astype(BF), expert_w_up[i].astype(BF),
                       expert_w_down[i].astype(BF), n_slots,
                       x1, mod, i, nctx, rows, ln_ffn_g[i], ln_ffn_b[i], g_lo)
    return xs
```

```python
import functools

import jax
import jax.numpy as jnp
import numpy as np
from jax import lax
from jax.experimental import pallas as pl
from jax.experimental.pallas import tpu as pltpu

D_MODEL = 1024
DEPTH = 4
GRID_W = 64
ROPE_THETA = 10000.0
NORM_EPS = 1e-6
GQA_HEADS = 8
GQA_KV_HEADS = 2
GQA_HEAD_DIM = 128
MLA_HEADS = 8
MLA_Q_RANK = 768
MLA_KV_RANK = 256
MLA_NOPE_DIM = 128
MLA_ROPE_DIM = 64
MLA_V_DIM = 128
N_EXPERTS = 16
EXPERT_FF = 1024
CAPACITY_FACTOR = 2
N_MIXERS = 2
DEEPNORM_ALPHA = (2 * DEPTH) ** 0.25
LOG2E = 1.4426950408889634

LANE = 128
MOD_ROWS_PAD = 8
ROUTE_GROUPS_PER_STEP = 4
MOE_TOKEN_CHUNK = 256
MOE_SLOT_BLOCK = 128
VMEM_LIMIT = 56 * 1024 * 1024

BF = jnp.bfloat16
F32 = jnp.float32

_NT = (((1,), (1,)), ((), ()))
_TN = (((0,), (0,)), ((), ()))


def _cparams(sem):
    return pltpu.CompilerParams(dimension_semantics=sem, vmem_limit_bytes=VMEM_LIMIT)


def _ada_kernel(c_ref, w_ref, b_ref, o_ref):
    c = c_ref[...]
    s = (c * jax.nn.sigmoid(c)).astype(BF)
    o_ref[0] = jnp.dot(s, w_ref[0].astype(BF), preferred_element_type=F32) + b_ref[0]


def _ada_call(cvec, ada_w, ada_b):
    n_layers, d, n_out = ada_w.shape
    rows = cvec.shape[0]
    tn = 1536
    return pl.pallas_call(
        _ada_kernel,
        grid=(n_layers, n_out // tn),
        in_specs=[
            pl.BlockSpec((rows, d), lambda i, j: (0, 0)),
            pl.BlockSpec((1, d, tn), lambda i, j: (i, 0, j)),
            pl.BlockSpec((1, 1, tn), lambda i, j: (i, 0, j)),
        ],
        out_specs=pl.BlockSpec((1, rows, tn), lambda i, j: (i, 0, j)),
        out_shape=jax.ShapeDtypeStruct((n_layers, rows, n_out), F32),
        compiler_params=_cparams(("parallel", "parallel")),
        name="ada_mod",
    )(cvec, ada_w, ada_b.reshape(n_layers, 1, n_out))


def _rope_tables(n, rot_dim):
    t = jnp.arange(n, dtype=jnp.int32)
    row = (t // GRID_W).astype(F32)
    col = (t % GRID_W).astype(F32)
    axis_dim = rot_dim // 2
    freqs = ROPE_THETA ** (-jnp.arange(0, axis_dim, 2, dtype=F32) / axis_dim)
    ang = jnp.concatenate([row[:, None] * freqs, col[:, None] * freqs], axis=-1)
    return jnp.cos(ang), jnp.sin(ang)


def _act_specs(x, nctx, tm, g_lo=0):
    if not isinstance(x, tuple):
        return [pl.BlockSpec((1, tm, x.shape[2]), lambda g, r: (g + g_lo, r, 0))], [x]
    xc, xl = x
    if g_lo == nctx:
        return [pl.BlockSpec((1, tm, xl.shape[2]), lambda g, r: (g, r, 0))], [xl]
    assert g_lo == 0
    last_r = xc.shape[1] // tm - 1
    spec_c = pl.BlockSpec((1, tm, xc.shape[2]),
                          lambda g, r: (jnp.minimum(g, nctx - 1), jnp.where(g < nctx, r, last_r), 0))
    spec_l = pl.BlockSpec((1, tm, xl.shape[2]),
                          lambda g, r: (jnp.maximum(g - nctx, 0), jnp.where(g < nctx, 0, r), 0))
    return [spec_c, spec_l], [xc, xl]


def _act_tile(x_refs, is_ctx):
    if len(x_refs) == 1:
        return x_refs[0][0]
    return jnp.where(is_ctx, x_refs[0][0], x_refs[1][0])


def _gqa_proj_kernel(*refs, nctx, n_x):
    x_refs, refs = refs[:n_x], refs[n_x:]
    (sh_ref, sc_ref, wqv_ref, wk_ref, gq_ref, gk_ref, cs_ref, sn_ref, cst_ref, snt_ref,
     qt_ref, k_ref, vt_ref) = refs
    is_ctx = pl.program_id(0) < nctx
    h = (_act_tile(x_refs, is_ctx) * (1.0 + sc_ref[0]) + sh_ref[0]).astype(BF)
    qv = lax.dot_general(wqv_ref[...], h, _NT, preferred_element_type=F32)
    cost = jnp.where(is_ctx, 1.0, cst_ref[...])
    sint = jnp.where(is_ctx, 0.0, snt_ref[...])
    hd, half = GQA_HEAD_DIM, GQA_HEAD_DIM // 2
    for i in range(GQA_HEADS):
        q = qv[i * hd:(i + 1) * hd]
        ms = jnp.mean(q * q, axis=0, keepdims=True)
        qn = q * lax.rsqrt(ms + NORM_EPS) * gq_ref[...]
        x0, x1 = qn[:half], qn[half:]
        qt_ref[0, i * hd:i * hd + half, :] = (x0 * cost - x1 * sint).astype(BF)
        qt_ref[0, i * hd + half:(i + 1) * hd, :] = (x0 * sint + x1 * cost).astype(BF)
    vt_ref[0] = qv[GQA_HEADS * hd:].astype(BF)
    kk = jnp.dot(h, wk_ref[...], preferred_element_type=F32)
    cos = jnp.where(is_ctx, 1.0, cs_ref[...])
    sin = jnp.where(is_ctx, 0.0, sn_ref[...])
    for j in range(GQA_KV_HEADS):
        k = kk[:, j * hd:(j + 1) * hd]
        ms = jnp.mean(k * k, axis=1, keepdims=True)
        kn = k * lax.rsqrt(ms + NORM_EPS) * gk_ref[...]
        k_ref[0, :, j * hd:(j + 1) * hd] = (kn * cos + pltpu.roll(kn, half, 1) * sin).astype(BF)


def _mod_spec(layer, chunk, nctx, rows_per_layer, d):
    def imap(g, *_):
        return (layer * rows_per_layer + jnp.where(g < nctx, 0, g - nctx + 1), 0, chunk)
    return pl.BlockSpec((1, 1, d), imap)


def _gqa_proj_call(x, mod, layer, nctx, rows_per_layer, w_qkv, q_g, k_g, tm=512):
    x_specs, x_args = _act_specs(x, nctx, tm)
    g_n = sum(a.shape[0] for a in x_args)
    _, n, d = x_args[0].shape
    hd, half = GQA_HEAD_DIM, GQA_HEAD_DIM // 2
    nq, nk = GQA_HEADS * hd, GQA_KV_HEADS * hd
    perm = np.concatenate([np.arange(0, hd, 2), np.arange(1, hd, 2)])
    wq = w_qkv[:, :nq].reshape(d, GQA_HEADS, hd)[:, :, perm].reshape(d, nq)
    wk = w_qkv[:, nq:nq + nk].reshape(d, GQA_KV_HEADS, hd)[:, :, perm].reshape(d, nk)
    wv = w_qkv[:, nq + nk:]
    wqv_t = jnp.concatenate([wq, wv], axis=1).T.astype(BF)
    wk = wk.astype(BF)
    gq = (q_g[perm] * (hd ** -0.5 * LOG2E)).reshape(hd, 1)
    gk = k_g[perm].reshape(1, hd)
    cos, sin = _rope_tables(n, hd)
    cs = jnp.concatenate([cos, cos], axis=1)
    sn = jnp.concatenate([-sin, sin], axis=1)
    kernel = functools.partial(_gqa_proj_kernel, nctx=nctx, n_x=len(x_args))
    full = lambda shape: pl.BlockSpec(shape, lambda g, r: (0,) * len(shape))
    return pl.pallas_call(
        kernel,
        grid=(g_n, n // tm),
        in_specs=x_specs + [
            _mod_spec(layer, 0, nctx, rows_per_layer, d),
            _mod_spec(layer, 1, nctx, rows_per_layer, d),
            full((nq + nk, d)),
            full((d, nk)),
            full((hd, 1)),
            full((1, hd)),
            pl.BlockSpec((tm, hd), lambda g, r: (r, 0)),
            pl.BlockSpec((tm, hd), lambda g, r: (r, 0)),
            pl.BlockSpec((half, tm), lambda g, r: (0, r)),
            pl.BlockSpec((half, tm), lambda g, r: (0, r)),
        ],
        out_specs=[
            pl.BlockSpec((1, nq, tm), lambda g, r: (g, 0, r)),
            pl.BlockSpec((1, tm, nk), lambda g, r: (g, r, 0)),
            pl.BlockSpec((1, nk, tm), lambda g, r: (g, 0, r)),
        ],
        out_shape=[
            jax.ShapeDtypeStruct((g_n, nq, n), BF),
            jax.ShapeDtypeStruct((g_n, n, nk), BF),
            jax.ShapeDtypeStruct((g_n, nk, n), BF),
        ],
        compiler_params=_cparams(("parallel", "parallel")),
        name="gqa_proj",
    )(*x_args, mod, mod, wqv_t, wk, gq, gk, cs, sn, cos.T, sin.T)


MLA_QK_PAD = 256


def _mla_proj_kernel(x_ref, sh_ref, sc_ref, wdq_ref, gq_ref, wuq_ref, wdkv_ref, gkv_ref,
                     wukn_ref, wuv_ref, cs_ref, sn_ref, cst_ref, snt_ref,
                     qt_ref, kn_ref, kr_ref, vt_ref, *, nctx):
    is_ctx = pl.program_id(0) < nctx
    h = (x_ref[0] * (1.0 + sc_ref[0]) + sh_ref[0]).astype(BF)
    cq = jnp.dot(h, wdq_ref[...], preferred_element_type=F32)
    ms = jnp.mean(cq * cq, axis=1, keepdims=True)
    cqn = (cq * lax.rsqrt(ms + NORM_EPS) * gq_ref[...]).astype(BF)
    qt = lax.dot_general(wuq_ref[...], cqn, _NT, preferred_element_type=F32)
    qt = qt * ((MLA_NOPE_DIM + MLA_ROPE_DIM) ** -0.5 * LOG2E)
    cost = jnp.where(is_ctx, 1.0, cst_ref[...])
    sint = jnp.where(is_ctx, 0.0, snt_ref[...])
    hr = MLA_ROPE_DIM // 2
    for i in range(MLA_HEADS):
        base = i * MLA_QK_PAD
        qt_ref[0, base:base + MLA_NOPE_DIM, :] = qt[base:base + MLA_NOPE_DIM].astype(BF)
        r0 = base + MLA_NOPE_DIM
        x0 = qt[r0:r0 + hr]
        x1 = qt[r0 + 2 * hr:r0 + 3 * hr]
        zero = jnp.zeros_like(x0).astype(BF)
        qt_ref[0, r0:r0 + hr, :] = (x0 * cost - x1 * sint).astype(BF)
        qt_ref[0, r0 + hr:r0 + 2 * hr, :] = zero
        qt_ref[0, r0 + 2 * hr:r0 + 3 * hr, :] = (x0 * sint + x1 * cost).astype(BF)
        qt_ref[0, r0 + 3 * hr:r0 + 4 * hr, :] = zero
    ckv = jnp.dot(h, wdkv_ref[...], preferred_element_type=F32)
    c = ckv[:, :MLA_KV_RANK]
    ms = jnp.mean(c * c, axis=1, keepdims=True)
    cn = (c * lax.rsqrt(ms + NORM_EPS) * gkv_ref[...]).astype(BF)
    kr = ckv[:, MLA_KV_RANK:]
    cos = jnp.where(is_ctx, 1.0, cs_ref[...])
    sin = jnp.where(is_ctx, 0.0, sn_ref[...])
    kr_ref[0] = (kr * cos + pltpu.roll(kr, LANE // 2, 1) * sin).astype(BF)
    kn_ref[0] = jnp.dot(cn, wukn_ref[...], preferred_element_type=F32).astype(BF)
    vt_ref[0] = lax.dot_general(wuv_ref[...], cn, _NT, preferred_element_type=F32).astype(BF)


def _mla_proj_call(x, mod, layer, nctx, rows_per_layer, w_dq, q_g, w_uq, w_dkv, kv_g, w_ukv, tm=512):
    g_n, n, d = x.shape
    nh, dn, dr, dv = MLA_HEADS, MLA_NOPE_DIM, MLA_ROPE_DIM, MLA_V_DIM
    hr = dr // 2
    ev, od = np.arange(0, dr, 2), np.arange(1, dr, 2)
    wu = w_uq.reshape(MLA_Q_RANK, nh, dn + dr)
    zq = jnp.zeros((MLA_Q_RANK, nh, hr), F32)
    wu = jnp.concatenate([wu[:, :, :dn], wu[:, :, dn + ev], zq, wu[:, :, dn + od], zq], axis=2)
    wuq_t = wu.reshape(MLA_Q_RANK, nh * MLA_QK_PAD).T.astype(BF)
    zk = jnp.zeros((d, hr), F32)
    wdkv = jnp.concatenate([w_dkv[:, :MLA_KV_RANK], w_dkv[:, MLA_KV_RANK + ev], zk,
                            w_dkv[:, MLA_KV_RANK + od], zk], axis=1).astype(BF)
    wkv = w_ukv.reshape(MLA_KV_RANK, nh, dn + dv)
    wukn = wkv[:, :, :dn].reshape(MLA_KV_RANK, nh * dn).astype(BF)
    wuv_t = wkv[:, :, dn:].reshape(MLA_KV_RANK, nh * dv).T.astype(BF)
    cos, sin = _rope_tables(n, dr)
    z = jnp.zeros_like(cos)
    cs = jnp.concatenate([cos, z, cos, z], axis=1)
    sn = jnp.concatenate([-sin, z, sin, z], axis=1)
    kernel = functools.partial(_mla_proj_kernel, nctx=nctx)
    full = lambda shape: pl.BlockSpec(shape, lambda g, r: (0,) * len(shape))
    return pl.pallas_call(
        kernel,
        grid=(g_n, n // tm),
        in_specs=[
            pl.BlockSpec((1, tm, d), lambda g, r: (g, r, 0)),
            _mod_spec(layer, 0, nctx, rows_per_layer, d),
            _mod_spec(layer, 1, nctx, rows_per_layer, d),
            full((d, MLA_Q_RANK)),
            full((1, MLA_Q_RANK)),
            full((nh * MLA_QK_PAD, MLA_Q_RANK)),
            full((d, MLA_KV_RANK + LANE)),
            full((1, MLA_KV_RANK)),
            full((MLA_KV_RANK, nh * dn)),
            full((nh * dv, MLA_KV_RANK)),
            pl.BlockSpec((tm, LANE), lambda g, r: (r, 0)),
            pl.BlockSpec((tm, LANE), lambda g, r: (r, 0)),
            pl.BlockSpec((hr, tm), lambda g, r: (0, r)),
            pl.BlockSpec((hr, tm), lambda g, r: (0, r)),
        ],
        out_specs=[
            pl.BlockSpec((1, nh * MLA_QK_PAD, tm), lambda g, r: (g, 0, r)),
            pl.BlockSpec((1, tm, nh * dn), lambda g, r: (g, r, 0)),
            pl.BlockSpec((1, tm, LANE), lambda g, r: (g, r, 0)),
            pl.BlockSpec((1, nh * dv, tm), lambda g, r: (g, 0, r)),
        ],
        out_shape=[
            jax.ShapeDtypeStruct((g_n, nh * MLA_QK_PAD, n), BF),
            jax.ShapeDtypeStruct((g_n, n, nh * dn), BF),
            jax.ShapeDtypeStruct((g_n, n, LANE), BF),
            jax.ShapeDtypeStruct((g_n, nh * dv, n), BF),
        ],
        compiler_params=_cparams(("parallel", "parallel")),
        name="mla_proj",
    )(x, mod, mod, w_dq.astype(BF), q_g.reshape(1, -1), wuq_t, wdkv, kv_g.reshape(1, -1),
      wukn, wuv_t, cs, sn, cos.T, sin.T)


SUBLANE = 8


def _attn_kernel(*refs, mla, with_lat, hb, rep, dq, dv, ck, sb):
    refs = list(refs)
    qt_ref = refs.pop(0)
    kl_ref = refs.pop(0) if with_lat else None
    kc_ref = refs.pop(0)
    krl_ref = refs.pop(0) if (mla and with_lat) else None
    krc_ref = refs.pop(0) if mla else None
    vtl_ref = refs.pop(0) if with_lat else None
    vtc_ref = refs.pop(0)
    p_scr = refs.pop()
    s_scr = refs.pop()
    ot_ref = refs.pop()

    def pipeline(q0, tq, chunks):
        def stage_a(u, chunk):
            k_ref, kr_ref, _, r0, rows, s0 = chunk
            j = u // rep
            keys = k_ref[0, r0:r0 + rows, j * LANE:(j + 1) * LANE]
            if mla:
                keys = jnp.concatenate([keys, kr_ref[0, r0:r0 + rows, :]], axis=1)
            s = jnp.dot(keys, qt_ref[0, u * dq:(u + 1) * dq, q0:q0 + tq],
                        preferred_element_type=F32)
            s_scr[u % 2, s0:s0 + rows, :] = s
            return jnp.max(s.reshape(rows // SUBLANE, SUBLANE, tq), axis=0)

        def stage_b(u, ci, chunk, m):
            _, _, vt_ref, r0, rows, s0 = chunk
            j = u // rep
            l8 = None
            for r in range(0, rows, sb):
                p = jnp.exp2(s_scr[u % 2, s0 + r:s0 + r + sb, :] - m)
                ps = jnp.sum(p.reshape(sb // SUBLANE, SUBLANE, tq), axis=0)
                l8 = ps if l8 is None else l8 + ps
                p_scr[ci % 2, r:r + sb, :] = p.astype(BF)
            pv = jnp.dot(vt_ref[0, j * dv:(j + 1) * dv, r0:r0 + rows], p_scr[ci % 2, :rows, :],
                         preferred_element_type=F32)
            return l8, pv

        m_cur = None
        for u in range(hb + 1):
            m_parts, l_parts, acc = [], [], None
            for ci, chunk in enumerate(chunks):
                if u < hb:
                    m_parts.append(stage_a(u, chunk))
                if u >= 1:
                    l8, pv = stage_b(u - 1, ci, chunk, m_cur)
                    l_parts.append(l8)
                    acc = pv if acc is None else acc + pv
            if u >= 1:
                l = jnp.sum(functools.reduce(jnp.add, l_parts), axis=0, keepdims=True)
                ot_ref[0, (u - 1) * dv:u * dv, q0:q0 + tq] = (acc / l).astype(BF)
            if u < hb:
                m_cur = jnp.max(functools.reduce(jnp.maximum, m_parts), axis=0, keepdims=True)

    if with_lat:
        n_ctx = kc_ref.shape[1]
        chunks = [(kc_ref, krc_ref, vtc_ref, 0, n_ctx, 0)]
        for r0 in range(0, kl_ref.shape[1], ck):
            chunks.append((kl_ref, krl_ref, vtl_ref, r0, ck, n_ctx + r0))
        pipeline(0, qt_ref.shape[2], chunks)
    else:
        n_ctx = s_scr.shape[1]
        for s in range(kc_ref.shape[1] // n_ctx):
            pipeline(s * n_ctx, n_ctx, [(kc_ref, krc_ref, vtc_ref, s * n_ctx, n_ctx, 0)])


def _attn_call(qt, k, kr, vt, nctx, n_ctx_len, n_heads, n_kv_heads, with_ctx_out, tq=512, ck=1024, sb=32):
    g_n, _, n = qt.shape
    dq = qt.shape[1] // n_heads
    dv = vt.shape[1] // n_kv_heads
    batch = g_n - nctx
    spg = n // n_ctx_len
    rep = n_heads // n_kv_heads
    hb = n_heads
    hkv = hb // rep
    mla = kr is not None
    ot_shape = jax.ShapeDtypeStruct((g_n, n_heads * dv, n), BF)
    kern = functools.partial(_attn_kernel, mla=mla, hb=hb, rep=rep, dq=dq, dv=dv, ck=ck, sb=sb)

    lat_in = [pl.BlockSpec((1, hb * dq, tq), lambda b, h, t: (nctx + b, h, t)),
              pl.BlockSpec((1, n, hkv * LANE), lambda b, h, t: (nctx + b, 0, h)),
              pl.BlockSpec((1, n_ctx_len, hkv * LANE), lambda b, h, t: (b // spg, b % spg, h))]
    lat_args = [qt, k, k]
    if mla:
        lat_in += [pl.BlockSpec((1, n, LANE), lambda b, h, t: (nctx + b, 0, 0)),
                   pl.BlockSpec((1, n_ctx_len, LANE), lambda b, h, t: (b // spg, b % spg, 0))]
        lat_args += [kr, kr]
    lat_in += [pl.BlockSpec((1, hkv * dv, n), lambda b, h, t: (nctx + b, h, 0)),
               pl.BlockSpec((1, hkv * dv, n_ctx_len), lambda b, h, t: (b // spg, h, b % spg))]
    lat_args += [vt, vt]
    ot = pl.pallas_call(
        functools.partial(kern, with_lat=True),
        grid=(batch, n_heads // hb, n // tq),
        in_specs=lat_in,
        out_specs=pl.BlockSpec((1, hb * dv, tq), lambda b, h, t: (nctx + b, h, t)),
        out_shape=ot_shape,
        scratch_shapes=[pltpu.VMEM((2, n_ctx_len + n, tq), F32),
                        pltpu.VMEM((2, max(ck, n_ctx_len), tq), BF)],
        compiler_params=_cparams(("parallel", "parallel", "parallel")),
        name="attn_latent",
    )(*lat_args)
    if not with_ctx_out:
        return ot

    ctx_in = [pl.BlockSpec((1, hb * dq, n), lambda g: (g, 0, 0)),
              pl.BlockSpec((1, n, hkv * LANE), lambda g: (g, 0, 0))]
    ctx_args = [qt, k]
    if mla:
        ctx_in += [pl.BlockSpec((1, n, LANE), lambda g: (g, 0, 0))]
        ctx_args += [kr]
    ctx_in += [pl.BlockSpec((1, hkv * dv, n), lambda g: (g, 0, 0)),
               pl.BlockSpec(memory_space=pl.ANY)]
    ctx_args += [vt, ot]
    return pl.pallas_call(
        functools.partial(kern, with_lat=False),
        grid=(nctx,),
        in_specs=ctx_in,
        out_specs=pl.BlockSpec((1, hb * dv, n), lambda g: (g, 0, 0)),
        out_shape=ot_shape,
        input_output_aliases={len(ctx_args) - 1: 0},
        scratch_shapes=[pltpu.VMEM((2, n_ctx_len, n_ctx_len), F32),
                        pltpu.VMEM((2, n_ctx_len, n_ctx_len), BF)],
        compiler_params=_cparams(("parallel",)),
        name="attn_ctx",
    )(*ctx_args)


def _layer_norm(z, g, b):
    mu = jnp.mean(z, axis=-1, keepdims=True)
    zc = z - mu
    var = jnp.mean(zc * zc, axis=-1, keepdims=True)
    return zc * lax.rsqrt(var + NORM_EPS) * g + b


def _oproj_kernel(*refs, nctx, g_lo, n_x):
    x_refs, refs = refs[:n_x], refs[n_x:]
    (ot_ref, wo_ref, g1_ref, lg_ref, lb_ref, sh2_ref, sc2_ref, wrh_ref, wrl_ref,
     x1_ref, h2_ref, aff_ref) = refs
    x = _act_tile(x_refs, pl.program_id(0) + g_lo < nctx)
    y = lax.dot_general(ot_ref[0], wo_ref[...], _TN, preferred_element_type=F32)
    x1 = _layer_norm(DEEPNORM_ALPHA * x + g1_ref[0] * y, lg_ref[...], lb_ref[...])
    x1_ref[0] = x1
    h2 = x1 * (1.0 + sc2_ref[0]) + sh2_ref[0]
    hb = h2.astype(BF)
    h2_ref[0] = hb
    hl = (h2 - hb.astype(F32)).astype(BF)
    lg = (lax.dot_general(wrh_ref[...], hb, _NT, preferred_element_type=F32)
          + lax.dot_general(wrl_ref[...], hb, _NT, preferred_element_type=F32)
          + lax.dot_general(wrh_ref[...], hl, _NT, preferred_element_type=F32))
    e = jnp.exp(lg - jnp.max(lg, axis=0, keepdims=True))
    aff_ref[0] = e / jnp.sum(e, axis=0, keepdims=True)


def _oproj_call(ot, x, mod, layer, nctx, rows_per_layer, w_o, ln_g, ln_b, w_router, g_lo, tm=512):
    x_specs, x_args = _act_specs(x, nctx, tm, g_lo)
    g_n = ot.shape[0]
    _, n, d = x_args[0].shape
    n_e = w_router.shape[1]
    wr_t = w_router.T
    wrh = wr_t.astype(BF)
    wrl = (wr_t - wrh.astype(F32)).astype(BF)
    full = lambda shape: pl.BlockSpec(shape, lambda g, r: (0,) * len(shape))

    def mspec(chunk):
        def imap(g, r):
            gg = g + g_lo
            return (layer * rows_per_layer + jnp.where(gg < nctx, 0, gg - nctx + 1), 0, chunk)
        return pl.BlockSpec((1, 1, d), imap)

    return pl.pallas_call(
        functools.partial(_oproj_kernel, nctx=nctx, g_lo=g_lo, n_x=len(x_args)),
        grid=(g_n - g_lo, n // tm),
        in_specs=x_specs + [
            pl.BlockSpec((1, ot.shape[1], tm), lambda g, r: (g + g_lo, 0, r)),
            full(w_o.shape),
            mspec(2),
            full((1, d)), full((1, d)),
            mspec(3), mspec(4),
            full((n_e, d)), full((n_e, d)),
        ],
        out_specs=[
            pl.BlockSpec((1, tm, d), lambda g, r: (g, r, 0)),
            pl.BlockSpec((1, tm, d), lambda g, r: (g, r, 0)),
            pl.BlockSpec((1, n_e, tm), lambda g, r: (g, 0, r)),
        ],
        out_shape=[
            jax.ShapeDtypeStruct((g_n - g_lo, n, d), F32),
            jax.ShapeDtypeStruct((g_n - g_lo, n, d), BF),
            jax.ShapeDtypeStruct((g_n - g_lo, n_e, n), F32),
        ],
        compiler_params=_cparams(("parallel", "parallel")),
        name="oproj_norm_router",
    )(*x_args, ot, w_o.astype(BF), mod, ln_g.reshape(1, d), ln_b.reshape(1, d), mod, mod, wrh, wrl)


def _route_segments(aff, seg_len, cap, tri):
    n_e, n = aff.shape
    nseg = n // seg_len
    bits = pltpu.bitcast(aff, jnp.int32)
    lane = lax.broadcasted_iota(jnp.int32, (n_e, n), 1)
    assert seg_len & (seg_len - 1) == 0
    pos = lane & (seg_len - 1)

    def seg_count(mask):
        v = jnp.where(mask, 1.0, 0.0)
        parts = []
        for s in range(nseg):
            c = jnp.sum(v[:, s * seg_len:(s + 1) * seg_len], axis=1, keepdims=True)
            parts.append(jnp.broadcast_to(c, (n_e, seg_len)))
        return parts[0] if nseg == 1 else jnp.concatenate(parts, axis=1)

    def thr_step(i, thr):
        cand = thr | (jnp.int32(1) << (30 - i))
        return jnp.where(seg_count(bits >= cand) >= cap, cand, thr)

    thr = lax.fori_loop(0, 31, thr_step, jnp.zeros((n_e, n), jnp.int32))
    gt = bits > thr
    eq = bits == thr
    need = cap - seg_count(gt)

    def tie_step(i, j):
        cand = j | (jnp.int32(1) << (seg_len.bit_length() - 2 - i))
        return jnp.where(seg_count(eq & (pos < cand)) < need, cand, j)

    j = lax.fori_loop(0, seg_len.bit_length() - 1, tie_step, jnp.zeros((n_e, n), jnp.int32))
    sel = gt | (eq & (pos <= j))

    selb = jnp.where(sel, 1.0, 0.0).astype(BF)
    tiles_per_seg = seg_len // LANE
    tiles_per_chunk = MOE_TOKEN_CHUNK // LANE
    lane1 = lax.broadcasted_iota(jnp.int32, (n_e, LANE), 1)
    starts = jnp.zeros((n_e, LANE), F32)
    out = []
    run = jnp.zeros((n_e, 1), F32)
    for t in range(n // LANE):
        if t % tiles_per_seg == 0:
            run = jnp.full((n_e, 1), float((t // tiles_per_seg) * cap), F32)
        if t % tiles_per_chunk == 0:
            starts = jnp.where(lane1 == t // tiles_per_chunk, run, starts)
        inc = jnp.dot(selb[:, t * LANE:(t + 1) * LANE], tri, preferred_element_type=F32)
        out.append(inc + (run - 1.0))
        run = run + inc[:, LANE - 1:LANE]
    starts = jnp.where(lane1 == n // MOE_TOKEN_CHUNK, run, starts)
    slot = jnp.concatenate(out, axis=1).astype(jnp.int32)
    return jnp.where(sel, slot, -1), starts.astype(jnp.int32)


def _route_kernel(aff_ref, tri_ref, slot_ref, starts_ref, *, nctx, n_ctx_len, g_lo):
    n = aff_ref.shape[2]
    gb, n_e = aff_ref.shape[0], aff_ref.shape[1]
    g = pl.program_id(0) * gb + g_lo

    def run(seg_len):
        slot, starts = _route_segments(aff_ref[...].reshape(gb * n_e, n), seg_len,
                                       CAPACITY_FACTOR * seg_len // N_EXPERTS, tri_ref[...])
        slot_ref[...] = slot.reshape(gb, n_e, n)
        starts_ref[...] = starts.reshape(gb, n_e, LANE)

    if g_lo < nctx:
        pl.when(g < nctx)(functools.partial(run, n_ctx_len))
    pl.when(g >= nctx)(functools.partial(run, n))


def _route_call(aff, nctx, n_ctx_len, g_lo):
    g_n, n_e, n = aff.shape
    assert n_ctx_len % MOE_TOKEN_CHUNK == 0 and n // MOE_TOKEN_CHUNK < LANE
    gb = ROUTE_GROUPS_PER_STEP
    if (nctx - g_lo) % gb or g_n % gb:
        gb = 1
    tri = jnp.triu(jnp.ones((LANE, LANE), F32)).astype(BF)
    return pl.pallas_call(
        functools.partial(_route_kernel, nctx=nctx, n_ctx_len=n_ctx_len, g_lo=g_lo),
        grid=(g_n // gb,),
        in_specs=[pl.BlockSpec((gb, n_e, n), lambda g: (g, 0, 0)),
                  pl.BlockSpec((LANE, LANE), lambda g: (0, 0))],
        out_specs=[pl.BlockSpec((gb, n_e, n), lambda g: (g, 0, 0)),
                   pl.BlockSpec((gb, n_e, LANE), lambda g: (g, 0, 0))],
        out_shape=[jax.ShapeDtypeStruct((g_n, n_e, n), jnp.int32),
                   jax.ShapeDtypeStruct((g_n, n_e, LANE), jnp.int32)],
        compiler_params=_cparams(("parallel",)),
        name="ec_route",
    )(aff, tri)


def _gather_pairs(starts, n_slots, n_chunks):
    n_blocks = n_slots // MOE_SLOT_BLOCK
    n_pairs = n_blocks + n_chunks - 1
    assert n_blocks <= 16 and n_chunks <= 16
    s = starts[..., :n_chunks + 1]
    lo = jnp.arange(n_blocks, dtype=jnp.int32) * MOE_SLOT_BLOCK
    c_lo = jnp.sum(s[..., None, :] <= lo[:, None], axis=-1) - 1
    c_hi = jnp.sum(s[..., None, :] <= lo[:, None] + (MOE_SLOT_BLOCK - 1), axis=-1) - 1
    cnt = c_hi - c_lo + 1
    first = jnp.cumsum(cnt, axis=-1) - cnt
    j = jnp.arange(n_pairs, dtype=jnp.int32)
    r = jnp.sum(first[..., None, :] <= j[:, None], axis=-1) - 1
    pick = r[..., None] == jnp.arange(n_blocks, dtype=jnp.int32)
    c = j + jnp.sum(jnp.where(pick, (c_lo - first)[..., None, :], 0), axis=-1)
    valid = j < jnp.sum(cnt, axis=-1, keepdims=True)
    return jnp.where(valid, 256 + r * 16 + c, 0).astype(jnp.int32).reshape(-1), n_pairs


def _moe_kernel(pairs_ref, h_ref, slot_ref, gate_ref, wg_ref, wu_ref, wd_ref,
                x1_ref, g2_ref, lg_ref, lb_ref, x2_ref, xs_scr, *, n_slots, n_pairs):
    g, e = pl.program_id(0), pl.program_id(1)
    n_e = pl.num_programs(1)
    n_chunks, chunk = slot_ref.shape[1], slot_ref.shape[2]
    sb = MOE_SLOT_BLOCK

    @pl.when(e == 0)
    def _():
        x2_ref[...] = jnp.zeros_like(x2_ref)

    xs_scr[...] = jnp.zeros_like(xs_scr)
    base = (g * n_e + e) * n_pairs
    row_iota = lax.broadcasted_iota(jnp.int32, (sb, chunk), 0)
    for j in range(n_pairs):
        code = pairs_ref[base + j]
        c = code & 15
        r = (code >> 4) & 15
        rel = jnp.where(code >= 256, slot_ref[0, pl.ds(c, 1), :] - r * sb, -1)
        onehot = jnp.where(row_iota == rel, 1.0, 0.0).astype(BF)
        tok0 = pl.multiple_of(c * chunk, chunk)
        s0 = pl.multiple_of(r * sb, sb)
        xs_scr[pl.ds(s0, sb), :] += jnp.dot(onehot, h_ref[0, pl.ds(tok0, chunk), :],
                                            preferred_element_type=F32)
    xs = xs_scr[...].astype(BF)

    a = jnp.dot(xs, wg_ref[0, 0], preferred_element_type=F32)
    u = jnp.dot(xs, wu_ref[0, 0], preferred_element_type=F32)
    hh = (a * jax.nn.sigmoid(a) * u).astype(BF)
    y = jnp.dot(hh, wd_ref[0, 0], preferred_element_type=F32).astype(BF)

    slot_iota = lax.broadcasted_iota(jnp.int32, (n_slots, chunk), 0)
    for c in range(n_chunks):
        rows = slice(c * chunk, (c + 1) * chunk)
        selg = jnp.where(slot_iota == slot_ref[0, c:c + 1, :], gate_ref[0, c:c + 1, :], 0.0).astype(BF)
        x2_ref[0, rows, :] += lax.dot_general(selg, y, _TN, preferred_element_type=F32)

    @pl.when(e == n_e - 1)
    def _():
        for c in range(n_chunks):
            rows = slice(c * chunk, (c + 1) * chunk)
            x2_ref[0, rows, :] = _layer_norm(DEEPNORM_ALPHA * x1_ref[0, rows, :] + g2_ref[0] * x2_ref[0, rows, :],
                                             lg_ref[...], lb_ref[...])


def _moe_call(h2, slot, starts, aff, w_gate, w_up, w_down, n_slots,
              x1, mod, layer, nctx, rows_per_layer, ln_g, ln_b, g_lo):
    g_n, n, d = h2.shape
    _, n_e, _, ff = w_gate.shape
    n_chunks = n // MOE_TOKEN_CHUNK
    slot3 = slot.reshape(g_n * n_e, n_chunks, MOE_TOKEN_CHUNK)
    gate3 = aff.reshape(g_n * n_e, n_chunks, MOE_TOKEN_CHUNK)
    pairs, n_pairs = _gather_pairs(starts, n_slots, n_chunks)

    def mod_map(g, e, p):
        gg = g + g_lo
        return (layer * rows_per_layer + jnp.where(gg < nctx, 0, gg - nctx + 1), 0, 5)

    full = lambda shape: pl.BlockSpec(shape, lambda g, e, p: (0,) * len(shape))
    grid_spec = pltpu.PrefetchScalarGridSpec(
        num_scalar_prefetch=1,
        grid=(g_n, n_e),
        in_specs=[
            pl.BlockSpec((1, n, d), lambda g, e, p: (g, 0, 0)),
            pl.BlockSpec((1, n_chunks, MOE_TOKEN_CHUNK), lambda g, e, p: (g * n_e + e, 0, 0)),
            pl.BlockSpec((1, n_chunks, MOE_TOKEN_CHUNK), lambda g, e, p: (g * n_e + e, 0, 0)),
            pl.BlockSpec((1, 1, d, ff), lambda g, e, p: (layer, e, 0, 0)),
            pl.BlockSpec((1, 1, d, ff), lambda g, e, p: (layer, e, 0, 0)),
            pl.BlockSpec((1, 1, ff, d), lambda g, e, p: (layer, e, 0, 0)),
            pl.BlockSpec((1, n, d), lambda g, e, p: (g, 0, 0), pipeline_mode=pl.Buffered(1)),
            pl.BlockSpec((1, 1, d), mod_map),
            full((1, d)), full((1, d)),
        ],
        out_specs=pl.BlockSpec((1, n, d), lambda g, e, p: (g, 0, 0)),
        scratch_shapes=[pltpu.VMEM((n_slots, d), F32)],
    )
    return pl.pallas_call(
        functools.partial(_moe_kernel, n_slots=n_slots, n_pairs=n_pairs),
        grid_spec=grid_spec,
        out_shape=jax.ShapeDtypeStruct((g_n, n, d), F32),
        compiler_params=_cparams(("parallel", "arbitrary")),
        name="ec_moe",
    )(pairs, h2, slot3, gate3, w_gate, w_up, w_down, x1, mod, ln_g.reshape(1, d), ln_b.reshape(1, d))


def kernel(x, c, ctx, c_ctx, ada_w, ada_b, ln_mix_g, ln_mix_b, ln_ffn_g, ln_ffn_b, router_w, expert_w_gate, expert_w_up, expert_w_down, gqa_w_qkv, gqa_q_g, gqa_k_g, gqa_w_o, mla_w_dq, mla_q_g, mla_w_uq, mla_w_dkv, mla_kv_g, mla_w_ukv, mla_w_o):
    batch, n, d = x.shape
    n_ctx_len = ctx.shape[1]
    depth = ada_w.shape[0]
    assert n % n_ctx_len == 0 and (batch * n_ctx_len) % n == 0
    nctx = batch * n_ctx_len // n
    n_slots = CAPACITY_FACTOR * n // N_EXPERTS

    rows = -(-(batch + 1) // MOD_ROWS_PAD) * MOD_ROWS_PAD
    cvec = jnp.concatenate([c_ctx[None, :], c, jnp.zeros((rows - batch - 1, d), F32)], axis=0)
    mod = _ada_call(cvec, ada_w, ada_b).reshape(depth * rows, 1, 6 * d)

    w_gate, w_up, w_down = (w.astype(BF) for w in (expert_w_gate, expert_w_up, expert_w_down))
    xs = (ctx.reshape(nctx, n, d), x)
    for i in range(depth):
        last = i == depth - 1
        j = i // N_MIXERS
        if i % N_MIXERS == 0:
            qt, k, vt = _gqa_proj_call(xs, mod, i, nctx, rows, gqa_w_qkv[j], gqa_q_g[j], gqa_k_g[j])
            ot = _attn_call(qt, k, None, vt, nctx, n_ctx_len, GQA_HEADS, GQA_KV_HEADS, not last)
            w_o = gqa_w_o[j]
        else:
            qt, kn, kr, vt = _mla_proj_call(xs, mod, i, nctx, rows, mla_w_dq[j], mla_q_g[j], mla_w_uq[j],
                                            mla_w_dkv[j], mla_kv_g[j], mla_w_ukv[j])
            ot = _attn_call(qt, kn, kr, vt, nctx, n_ctx_len, MLA_HEADS, MLA_HEADS, not last)
            w_o = mla_w_o[j]
        g_lo = nctx if last else 0
        x1, h2, aff = _oproj_call(ot, xs, mod, i, nctx, rows, w_o, ln_mix_g[i], ln_mix_b[i],
                                  router_w[i], g_lo)
        slot, starts = _route_call(aff, nctx, n_ctx_len, g_lo)
        xs = _moe_call(h2, slot, starts, aff, w_gate, w_up, w_down, n_slots,
                       x1, mod, i, nctx, rows, ln_ffn_g[i], ln_ffn_b[i], g_lo)
    return xs
```

```python
import functools

import jax
import jax.numpy as jnp
import numpy as np
from jax import lax
from jax.experimental import pallas as pl
from jax.experimental.pallas import tpu as pltpu

D_MODEL = 1024
DEPTH = 4
GRID_W = 64
ROPE_THETA = 10000.0
NORM_EPS = 1e-6
GQA_HEADS = 8
GQA_KV_HEADS = 2
GQA_HEAD_DIM = 128
MLA_HEADS = 8
MLA_Q_RANK = 768
MLA_KV_RANK = 256
MLA_NOPE_DIM = 128
MLA_ROPE_DIM = 64
MLA_V_DIM = 128
N_EXPERTS = 16
EXPERT_FF = 1024
CAPACITY_FACTOR = 2
N_MIXERS = 2
DEEPNORM_ALPHA = (2 * DEPTH) ** 0.25
LOG2E = 1.4426950408889634

LANE = 128
MOD_ROWS_PAD = 8
ROUTE_GROUPS_PER_STEP = 4
MOE_TOKEN_CHUNK = 256
MOE_SLOT_BLOCK = 128
VMEM_LIMIT = 58 * 1024 * 1024

BF = jnp.bfloat16
F32 = jnp.float32

_NT = (((1,), (1,)), ((), ()))
_TN = (((0,), (0,)), ((), ()))


def _cparams(sem):
    return pltpu.CompilerParams(dimension_semantics=sem, vmem_limit_bytes=VMEM_LIMIT)


def _ada_kernel(c_ref, w_ref, b_ref, o_ref):
    c = c_ref[...]
    s = (c * jax.nn.sigmoid(c)).astype(BF)
    o_ref[0] = jnp.dot(s, w_ref[0].astype(BF), preferred_element_type=F32) + b_ref[0]


def _ada_call(cvec, ada_w, ada_b):
    n_layers, d, n_out = ada_w.shape
    rows = cvec.shape[0]
    tn = 1536
    return pl.pallas_call(
        _ada_kernel,
        grid=(n_layers, n_out // tn),
        in_specs=[
            pl.BlockSpec((rows, d), lambda i, j: (0, 0)),
            pl.BlockSpec((1, d, tn), lambda i, j: (i, 0, j)),
            pl.BlockSpec((1, 1, tn), lambda i, j: (i, 0, j)),
        ],
        out_specs=pl.BlockSpec((1, rows, tn), lambda i, j: (i, 0, j)),
        out_shape=jax.ShapeDtypeStruct((n_layers, rows, n_out), F32),
        compiler_params=_cparams(("parallel", "parallel")),
        name="ada_mod",
    )(cvec, ada_w, ada_b.reshape(n_layers, 1, n_out))


def _rope_tables(n, rot_dim):
    t = jnp.arange(n, dtype=jnp.int32)
    row = (t // GRID_W).astype(F32)
    col = (t % GRID_W).astype(F32)
    axis_dim = rot_dim // 2
    freqs = ROPE_THETA ** (-jnp.arange(0, axis_dim, 2, dtype=F32) / axis_dim)
    ang = jnp.concatenate([row[:, None] * freqs, col[:, None] * freqs], axis=-1)
    return jnp.cos(ang), jnp.sin(ang)


def _act_specs(x, nctx, tm, g_lo=0):
    if not isinstance(x, tuple):
        return [pl.BlockSpec((1, tm, x.shape[2]), lambda g, r: (g + g_lo, r, 0))], [x]
    xc, xl = x
    if g_lo == nctx:
        return [pl.BlockSpec((1, tm, xl.shape[2]), lambda g, r: (g, r, 0))], [xl]
    assert g_lo == 0
    last_r = xc.shape[1] // tm - 1
    spec_c = pl.BlockSpec((1, tm, xc.shape[2]),
                          lambda g, r: (jnp.minimum(g, nctx - 1), jnp.where(g < nctx, r, last_r), 0))
    spec_l = pl.BlockSpec((1, tm, xl.shape[2]),
                          lambda g, r: (jnp.maximum(g - nctx, 0), jnp.where(g < nctx, 0, r), 0))
    return [spec_c, spec_l], [xc, xl]


def _act_tile(x_refs, is_ctx):
    if len(x_refs) == 1:
        return x_refs[0][0]
    return jnp.where(is_ctx, x_refs[0][0], x_refs[1][0])


def _act_rows(x_refs, is_ctx, rows):
    if len(x_refs) == 1:
        return x_refs[0][0, rows, :]
    return jnp.where(is_ctx, x_refs[0][0, rows, :], x_refs[1][0, rows, :])


def _gqa_proj_kernel(*refs, nctx, n_x, sub):
    x_refs, refs = refs[:n_x], refs[n_x:]
    (sh_ref, sc_ref, wqv_ref, wk_ref, gq_ref, gk_ref, cs_ref, sn_ref, cst_ref, snt_ref,
     qt_ref, k_ref, vt_ref) = refs
    is_ctx = pl.program_id(0) < nctx
    hd, half = GQA_HEAD_DIM, GQA_HEAD_DIM // 2
    n_sub = k_ref.shape[1] // sub

    def project(i):
        rows = slice(i * sub, (i + 1) * sub)
        h = (_act_rows(x_refs, is_ctx, rows) * (1.0 + sc_ref[0]) + sh_ref[0]).astype(BF)
        qv = lax.dot_general(wqv_ref[...], h, _NT, preferred_element_type=F32)
        kk = jnp.dot(h, wk_ref[...], preferred_element_type=F32)
        return qv, kk

    def finish(i, qv, kk):
        rows = slice(i * sub, (i + 1) * sub)
        cost = jnp.where(is_ctx, 1.0, cst_ref[:, rows])
        sint = jnp.where(is_ctx, 0.0, snt_ref[:, rows])
        for u in range(GQA_HEADS):
            q = qv[u * hd:(u + 1) * hd]
            ms = jnp.mean(q * q, axis=0, keepdims=True)
            qn = q * lax.rsqrt(ms + NORM_EPS) * gq_ref[...]
            x0, x1 = qn[:half], qn[half:]
            qt_ref[0, u * hd:u * hd + half, rows] = (x0 * cost - x1 * sint).astype(BF)
            qt_ref[0, u * hd + half:(u + 1) * hd, rows] = (x0 * sint + x1 * cost).astype(BF)
        vt_ref[0, :, rows] = qv[GQA_HEADS * hd:].astype(BF)
        cos = jnp.where(is_ctx, 1.0, cs_ref[rows, :])
        sin = jnp.where(is_ctx, 0.0, sn_ref[rows, :])
        for j in range(GQA_KV_HEADS):
            k = kk[:, j * hd:(j + 1) * hd]
            ms = jnp.mean(k * k, axis=1, keepdims=True)
            kn = k * lax.rsqrt(ms + NORM_EPS) * gk_ref[...]
            k_ref[0, rows, j * hd:(j + 1) * hd] = (kn * cos + pltpu.roll(kn, half, 1) * sin).astype(BF)

    prev = None
    for i in range(n_sub + 1):
        cur = project(i) if i < n_sub else None
        if i >= 1:
            finish(i - 1, *prev)
        prev = cur


def _mod_spec(layer, chunk, nctx, rows_per_layer, d):
    def imap(g, *_):
        return (layer * rows_per_layer + jnp.where(g < nctx, 0, g - nctx + 1), 0, chunk)
    return pl.BlockSpec((1, 1, d), imap)


def _gqa_proj_call(x, mod, layer, nctx, rows_per_layer, w_qkv, q_g, k_g, tm=2048, sub=256):
    x_specs, x_args = _act_specs(x, nctx, tm)
    g_n = sum(a.shape[0] for a in x_args)
    _, n, d = x_args[0].shape
    hd, half = GQA_HEAD_DIM, GQA_HEAD_DIM // 2
    nq, nk = GQA_HEADS * hd, GQA_KV_HEADS * hd
    perm = np.concatenate([np.arange(0, hd, 2), np.arange(1, hd, 2)])
    wq = w_qkv[:, :nq].reshape(d, GQA_HEADS, hd)[:, :, perm].reshape(d, nq)
    wk = w_qkv[:, nq:nq + nk].reshape(d, GQA_KV_HEADS, hd)[:, :, perm].reshape(d, nk)
    wv = w_qkv[:, nq + nk:]
    wqv_t = jnp.concatenate([wq, wv], axis=1).T.astype(BF)
    wk = wk.astype(BF)
    gq = (q_g[perm] * (hd ** -0.5 * LOG2E)).reshape(hd, 1)
    gk = k_g[perm].reshape(1, hd)
    cos, sin = _rope_tables(n, hd)
    cs = jnp.concatenate([cos, cos], axis=1)
    sn = jnp.concatenate([-sin, sin], axis=1)
    kernel = functools.partial(_gqa_proj_kernel, nctx=nctx, n_x=len(x_args), sub=sub)
    full = lambda shape: pl.BlockSpec(shape, lambda g, r: (0,) * len(shape))
    return pl.pallas_call(
        kernel,
        grid=(g_n, n // tm),
        in_specs=x_specs + [
            _mod_spec(layer, 0, nctx, rows_per_layer, d),
            _mod_spec(layer, 1, nctx, rows_per_layer, d),
            full((nq + nk, d)),
            full((d, nk)),
            full((hd, 1)),
            full((1, hd)),
            pl.BlockSpec((tm, hd), lambda g, r: (r, 0)),
            pl.BlockSpec((tm, hd), lambda g, r: (r, 0)),
            pl.BlockSpec((half, tm), lambda g, r: (0, r)),
            pl.BlockSpec((half, tm), lambda g, r: (0, r)),
        ],
        out_specs=[
            pl.BlockSpec((1, nq, tm), lambda g, r: (g, 0, r)),
            pl.BlockSpec((1, tm, nk), lambda g, r: (g, r, 0)),
            pl.BlockSpec((1, nk, tm), lambda g, r: (g, 0, r)),
        ],
        out_shape=[
            jax.ShapeDtypeStruct((g_n, nq, n), BF),
            jax.ShapeDtypeStruct((g_n, n, nk), BF),
            jax.ShapeDtypeStruct((g_n, nk, n), BF),
        ],
        compiler_params=_cparams(("parallel", "parallel")),
        name="gqa_proj",
    )(*x_args, mod, mod, wqv_t, wk, gq, gk, cs, sn, cos.T, sin.T)


MLA_QK_PAD = 256


def _mla_proj_kernel(x_ref, sh_ref, sc_ref, wdq_ref, gq_ref, wuq_ref, wdkv_ref, gkv_ref,
                     wukn_ref, wuv_ref, cs_ref, sn_ref, cst_ref, snt_ref,
                     qt_ref, kn_ref, kr_ref, vt_ref, *, nctx, sub):
    is_ctx = pl.program_id(0) < nctx
    hr = MLA_ROPE_DIM // 2
    n_sub = kn_ref.shape[1] // sub

    def down(i):
        rows = slice(i * sub, (i + 1) * sub)
        h = (x_ref[0, rows, :] * (1.0 + sc_ref[0]) + sh_ref[0]).astype(BF)
        cq = jnp.dot(h, wdq_ref[...], preferred_element_type=F32)
        ckv = jnp.dot(h, wdkv_ref[...], preferred_element_type=F32)
        return cq, ckv

    def up(i, cq, ckv):
        rows = slice(i * sub, (i + 1) * sub)
        ms = jnp.mean(cq * cq, axis=1, keepdims=True)
        cqn = (cq * lax.rsqrt(ms + NORM_EPS) * gq_ref[...]).astype(BF)
        qt = lax.dot_general(wuq_ref[...], cqn, _NT, preferred_element_type=F32)
        c = ckv[:, :MLA_KV_RANK]
        ms = jnp.mean(c * c, axis=1, keepdims=True)
        cn = (c * lax.rsqrt(ms + NORM_EPS) * gkv_ref[...]).astype(BF)
        kr = ckv[:, MLA_KV_RANK:]
        cos = jnp.where(is_ctx, 1.0, cs_ref[rows, :])
        sin = jnp.where(is_ctx, 0.0, sn_ref[rows, :])
        kr_ref[0, rows, :] = (kr * cos + pltpu.roll(kr, LANE // 2, 1) * sin).astype(BF)
        kn_ref[0, rows, :] = jnp.dot(cn, wukn_ref[...], preferred_element_type=F32).astype(BF)
        vt_ref[0, :, rows] = lax.dot_general(wuv_ref[...], cn, _NT, preferred_element_type=F32).astype(BF)
        return qt

    def rope(i, qt):
        rows = slice(i * sub, (i + 1) * sub)
        qt = qt * ((MLA_NOPE_DIM + MLA_ROPE_DIM) ** -0.5 * LOG2E)
        cost = jnp.where(is_ctx, 1.0, cst_ref[:, rows])
        sint = jnp.where(is_ctx, 0.0, snt_ref[:, rows])
        for u in range(MLA_HEADS):
            base = u * MLA_QK_PAD
            qt_ref[0, base:base + MLA_NOPE_DIM, rows] = qt[base:base + MLA_NOPE_DIM].astype(BF)
            r0 = base + MLA_NOPE_DIM
            x0 = qt[r0:r0 + hr]
            x1 = qt[r0 + 2 * hr:r0 + 3 * hr]
            zero = jnp.zeros_like(x0).astype(BF)
            qt_ref[0, r0:r0 + hr, rows] = (x0 * cost - x1 * sint).astype(BF)
            qt_ref[0, r0 + hr:r0 + 2 * hr, rows] = zero
            qt_ref[0, r0 + 2 * hr:r0 + 3 * hr, rows] = (x0 * sint + x1 * cost).astype(BF)
            qt_ref[0, r0 + 3 * hr:r0 + 4 * hr, rows] = zero

    d_prev, q_prev = None, None
    for i in range(n_sub + 2):
        d_cur = down(i) if i < n_sub else None
        q_cur = up(i - 1, *d_prev) if 1 <= i <= n_sub else None
        if i >= 2:
            rope(i - 2, q_prev)
        d_prev, q_prev = d_cur, q_cur


def _mla_proj_call(x, mod, layer, nctx, rows_per_layer, w_dq, q_g, w_uq, w_dkv, kv_g, w_ukv,
                   tm=1024, sub=256):
    g_n, n, d = x.shape
    nh, dn, dr, dv = MLA_HEADS, MLA_NOPE_DIM, MLA_ROPE_DIM, MLA_V_DIM
    hr = dr // 2
    ev, od = np.arange(0, dr, 2), np.arange(1, dr, 2)
    wu = w_uq.reshape(MLA_Q_RANK, nh, dn + dr)
    zq = jnp.zeros((MLA_Q_RANK, nh, hr), F32)
    wu = jnp.concatenate([wu[:, :, :dn], wu[:, :, dn + ev], zq, wu[:, :, dn + od], zq], axis=2)
    wuq_t = wu.reshape(MLA_Q_RANK, nh * MLA_QK_PAD).T.astype(BF)
    zk = jnp.zeros((d, hr), F32)
    wdkv = jnp.concatenate([w_dkv[:, :MLA_KV_RANK], w_dkv[:, MLA_KV_RANK + ev], zk,
                            w_dkv[:, MLA_KV_RANK + od], zk], axis=1).astype(BF)
    wkv = w_ukv.reshape(MLA_KV_RANK, nh, dn + dv)
    wukn = wkv[:, :, :dn].reshape(MLA_KV_RANK, nh * dn).astype(BF)
    wuv_t = wkv[:, :, dn:].reshape(MLA_KV_RANK, nh * dv).T.astype(BF)
    cos, sin = _rope_tables(n, dr)
    z = jnp.zeros_like(cos)
    cs = jnp.concatenate([cos, z, cos, z], axis=1)
    sn = jnp.concatenate([-sin, z, sin, z], axis=1)
    kernel = functools.partial(_mla_proj_kernel, nctx=nctx, sub=sub)
    full = lambda shape: pl.BlockSpec(shape, lambda g, r: (0,) * len(shape))
    return pl.pallas_call(
        kernel,
        grid=(g_n, n // tm),
        in_specs=[
            pl.BlockSpec((1, tm, d), lambda g, r: (g, r, 0)),
            _mod_spec(layer, 0, nctx, rows_per_layer, d),
            _mod_spec(layer, 1, nctx, rows_per_layer, d),
            full((d, MLA_Q_RANK)),
            full((1, MLA_Q_RANK)),
            full((nh * MLA_QK_PAD, MLA_Q_RANK)),
            full((d, MLA_KV_RANK + LANE)),
            full((1, MLA_KV_RANK)),
            full((MLA_KV_RANK, nh * dn)),
            full((nh * dv, MLA_KV_RANK)),
            pl.BlockSpec((tm, LANE), lambda g, r: (r, 0)),
            pl.BlockSpec((tm, LANE), lambda g, r: (r, 0)),
            pl.BlockSpec((hr, tm), lambda g, r: (0, r)),
            pl.BlockSpec((hr, tm), lambda g, r: (0, r)),
        ],
        out_specs=[
            pl.BlockSpec((1, nh * MLA_QK_PAD, tm), lambda g, r: (g, 0, r)),
            pl.BlockSpec((1, tm, nh * dn), lambda g, r: (g, r, 0)),
            pl.BlockSpec((1, tm, LANE), lambda g, r: (g, r, 0)),
            pl.BlockSpec((1, nh * dv, tm), lambda g, r: (g, 0, r)),
        ],
        out_shape=[
            jax.ShapeDtypeStruct((g_n, nh * MLA_QK_PAD, n), BF),
            jax.ShapeDtypeStruct((g_n, n, nh * dn), BF),
            jax.ShapeDtypeStruct((g_n, n, LANE), BF),
            jax.ShapeDtypeStruct((g_n, nh * dv, n), BF),
        ],
        compiler_params=_cparams(("parallel", "parallel")),
        name="mla_proj",
    )(x, mod, mod, w_dq.astype(BF), q_g.reshape(1, -1), wuq_t, wdkv, kv_g.reshape(1, -1),
      wukn, wuv_t, cs, sn, cos.T, sin.T)


SUBLANE = 8


def _attn_kernel(*refs, mla, with_lat, hb, rep, dq, dv, ck, sb):
    refs = list(refs)
    qt_ref = refs.pop(0)
    kl_ref = refs.pop(0) if with_lat else None
    kc_ref = refs.pop(0)
    krl_ref = refs.pop(0) if (mla and with_lat) else None
    krc_ref = refs.pop(0) if mla else None
    vtl_ref = refs.pop(0) if with_lat else None
    vtc_ref = refs.pop(0)
    p_scr = refs.pop()
    s_scr = refs.pop()
    ot_ref = refs.pop()

    def pipeline(q0, tq, chunks):
        def stage_a(u, chunk):
            k_ref, kr_ref, _, r0, rows, s0 = chunk
            j = u // rep
            keys = k_ref[0, r0:r0 + rows, j * LANE:(j + 1) * LANE]
            if mla:
                keys = jnp.concatenate([keys, kr_ref[0, r0:r0 + rows, :]], axis=1)
            s = jnp.dot(keys, qt_ref[0, u * dq:(u + 1) * dq, q0:q0 + tq],
                        preferred_element_type=F32)
            s_scr[u % 2, s0:s0 + rows, :] = s
            return jnp.max(s.reshape(rows // SUBLANE, SUBLANE, tq), axis=0)

        def stage_b(u, ci, chunk, m):
            _, _, vt_ref, r0, rows, s0 = chunk
            j = u // rep
            l8 = None
            for r in range(0, rows, sb):
                p = jnp.exp2(s_scr[u % 2, s0 + r:s0 + r + sb, :] - m)
                ps = jnp.sum(p.reshape(sb // SUBLANE, SUBLANE, tq), axis=0)
                l8 = ps if l8 is None else l8 + ps
                p_scr[ci % 2, r:r + sb, :] = p.astype(BF)
            pv = jnp.dot(vt_ref[0, j * dv:(j + 1) * dv, r0:r0 + rows], p_scr[ci % 2, :rows, :],
                         preferred_element_type=F32)
            return l8, pv

        m_cur = None
        for u in range(hb + 1):
            m_parts, l_parts, acc = [], [], None
            for ci, chunk in enumerate(chunks):
                if u < hb:
                    m_parts.append(stage_a(u, chunk))
                if u >= 1:
                    l8, pv = stage_b(u - 1, ci, chunk, m_cur)
                    l_parts.append(l8)
                    acc = pv if acc is None else acc + pv
            if u >= 1:
                l = jnp.sum(functools.reduce(jnp.add, l_parts), axis=0, keepdims=True)
                ot_ref[0, (u - 1) * dv:u * dv, q0:q0 + tq] = (acc / l).astype(BF)
            if u < hb:
                m_cur = jnp.max(functools.reduce(jnp.maximum, m_parts), axis=0, keepdims=True)

    if with_lat:
        n_ctx = kc_ref.shape[1]
        chunks = [(kc_ref, krc_ref, vtc_ref, 0, n_ctx, 0)]
        for r0 in range(0, kl_ref.shape[1], ck):
            chunks.append((kl_ref, krl_ref, vtl_ref, r0, ck, n_ctx + r0))
        pipeline(0, qt_ref.shape[2], chunks)
    else:
        n_ctx = s_scr.shape[1]
        for s in range(kc_ref.shape[1] // n_ctx):
            pipeline(s * n_ctx, n_ctx, [(kc_ref, krc_ref, vtc_ref, s * n_ctx, n_ctx, 0)])


def _attn_call(qt, k, kr, vt, nctx, n_ctx_len, n_heads, n_kv_heads, with_ctx_out, tq=512, ck=1024, sb=32):
    g_n, _, n = qt.shape
    dq = qt.shape[1] // n_heads
    dv = vt.shape[1] // n_kv_heads
    batch = g_n - nctx
    spg = n // n_ctx_len
    rep = n_heads // n_kv_heads
    hb = n_heads
    hkv = hb // rep
    mla = kr is not None
    ot_shape = jax.ShapeDtypeStruct((g_n, n_heads * dv, n), BF)
    kern = functools.partial(_attn_kernel, mla=mla, hb=hb, rep=rep, dq=dq, dv=dv, ck=ck, sb=sb)

    lat_in = [pl.BlockSpec((1, hb * dq, tq), lambda b, h, t: (nctx + b, h, t)),
              pl.BlockSpec((1, n, hkv * LANE), lambda b, h, t: (nctx + b, 0, h)),
              pl.BlockSpec((1, n_ctx_len, hkv * LANE), lambda b, h, t: (b // spg, b % spg, h))]
    lat_args = [qt, k, k]
    if mla:
        lat_in += [pl.BlockSpec((1, n, LANE), lambda b, h, t: (nctx + b, 0, 0)),
                   pl.BlockSpec((1, n_ctx_len, LANE), lambda b, h, t: (b // spg, b % spg, 0))]
        lat_args += [kr, kr]
    lat_in += [pl.BlockSpec((1, hkv * dv, n), lambda b, h, t: (nctx + b, h, 0)),
               pl.BlockSpec((1, hkv * dv, n_ctx_len), lambda b, h, t: (b // spg, h, b % spg))]
    lat_args += [vt, vt]
    ot = pl.pallas_call(
        functools.partial(kern, with_lat=True),
        grid=(batch, n_heads // hb, n // tq),
        in_specs=lat_in,
        out_specs=pl.BlockSpec((1, hb * dv, tq), lambda b, h, t: (nctx + b, h, t)),
        out_shape=ot_shape,
        scratch_shapes=[pltpu.VMEM((2, n_ctx_len + n, tq), F32),
                        pltpu.VMEM((2, max(ck, n_ctx_len), tq), BF)],
        compiler_params=_cparams(("parallel", "parallel", "parallel")),
        name="attn_latent",
    )(*lat_args)
    if not with_ctx_out:
        return ot

    ctx_in = [pl.BlockSpec((1, hb * dq, n), lambda g: (g, 0, 0)),
              pl.BlockSpec((1, n, hkv * LANE), lambda g: (g, 0, 0))]
    ctx_args = [qt, k]
    if mla:
        ctx_in += [pl.BlockSpec((1, n, LANE), lambda g: (g, 0, 0))]
        ctx_args += [kr]
    ctx_in += [pl.BlockSpec((1, hkv * dv, n), lambda g: (g, 0, 0)),
               pl.BlockSpec(memory_space=pl.ANY)]
    ctx_args += [vt, ot]
    return pl.pallas_call(
        functools.partial(kern, with_lat=False),
        grid=(nctx,),
        in_specs=ctx_in,
        out_specs=pl.BlockSpec((1, hb * dv, n), lambda g: (g, 0, 0)),
        out_shape=ot_shape,
        input_output_aliases={len(ctx_args) - 1: 0},
        scratch_shapes=[pltpu.VMEM((2, n_ctx_len, n_ctx_len), F32),
                        pltpu.VMEM((2, n_ctx_len, n_ctx_len), BF)],
        compiler_params=_cparams(("parallel",)),
        name="attn_ctx",
    )(*ctx_args)


def _layer_norm(z, g, b):
    mu = jnp.mean(z, axis=-1, keepdims=True)
    zc = z - mu
    var = jnp.mean(zc * zc, axis=-1, keepdims=True)
    return zc * lax.rsqrt(var + NORM_EPS) * g + b


def _oproj_kernel(*refs, nctx, g_lo, n_x, sub):
    x_refs, refs = refs[:n_x], refs[n_x:]
    (ot_ref, wo_ref, g1_ref, lg_ref, lb_ref, sh2_ref, sc2_ref, wr_ref,
     x1_ref, h2_ref, aff_ref) = refs
    is_ctx = pl.program_id(0) + g_lo < nctx
    tm = x1_ref.shape[1]

    def project(i):
        return lax.dot_general(ot_ref[0, :, i * sub:(i + 1) * sub], wo_ref[...], _TN,
                               preferred_element_type=F32)

    def finish(i, y):
        rows = slice(i * sub, (i + 1) * sub)
        if len(x_refs) == 1:
            x = x_refs[0][0, rows, :]
        else:
            x = jnp.where(is_ctx, x_refs[0][0, rows, :], x_refs[1][0, rows, :])
        x1 = _layer_norm(DEEPNORM_ALPHA * x + g1_ref[0] * y, lg_ref[...], lb_ref[...])
        x1_ref[0, rows, :] = x1
        h2 = x1 * (1.0 + sc2_ref[0]) + sh2_ref[0]
        hb = h2.astype(BF)
        h2_ref[0, rows, :] = hb
        lg = lax.dot_general(wr_ref[...], hb, _NT, preferred_element_type=F32)
        e = jnp.exp(lg - jnp.max(lg, axis=0, keepdims=True))
        aff_ref[0, :, rows] = e / jnp.sum(e, axis=0, keepdims=True)

    y_prev = None
    for i in range(tm // sub + 1):
        y = project(i) if i < tm // sub else None
        if i >= 1:
            finish(i - 1, y_prev)
        y_prev = y


def _oproj_call(ot, x, mod, layer, nctx, rows_per_layer, w_o, ln_g, ln_b, w_router, g_lo,
                tm=1024, sub=128):
    x_specs, x_args = _act_specs(x, nctx, tm, g_lo)
    g_n = ot.shape[0]
    _, n, d = x_args[0].shape
    n_e = w_router.shape[1]
    wr_t = w_router.T.astype(BF)
    full = lambda shape: pl.BlockSpec(shape, lambda g, r: (0,) * len(shape))

    def mspec(chunk):
        def imap(g, r):
            gg = g + g_lo
            return (layer * rows_per_layer + jnp.where(gg < nctx, 0, gg - nctx + 1), 0, chunk)
        return pl.BlockSpec((1, 1, d), imap)

    return pl.pallas_call(
        functools.partial(_oproj_kernel, nctx=nctx, g_lo=g_lo, n_x=len(x_args), sub=sub),
        grid=(g_n - g_lo, n // tm),
        in_specs=x_specs + [
            pl.BlockSpec((1, ot.shape[1], tm), lambda g, r: (g + g_lo, 0, r)),
            full(w_o.shape),
            mspec(2),
            full((1, d)), full((1, d)),
            mspec(3), mspec(4),
            full((n_e, d)),
        ],
        out_specs=[
            pl.BlockSpec((1, tm, d), lambda g, r: (g, r, 0)),
            pl.BlockSpec((1, tm, d), lambda g, r: (g, r, 0)),
            pl.BlockSpec((1, n_e, tm), lambda g, r: (g, 0, r)),
        ],
        out_shape=[
            jax.ShapeDtypeStruct((g_n - g_lo, n, d), F32),
            jax.ShapeDtypeStruct((g_n - g_lo, n, d), BF),
            jax.ShapeDtypeStruct((g_n - g_lo, n_e, n), F32),
        ],
        compiler_params=_cparams(("parallel", "parallel")),
        name="oproj_norm_router",
    )(*x_args, ot, w_o.astype(BF), mod, ln_g.reshape(1, d), ln_b.reshape(1, d), mod, mod, wr_t)


def _route_segments(aff, seg_len, cap, tri):
    n_e, n = aff.shape
    nseg = n // seg_len
    bits = pltpu.bitcast(aff, jnp.int32)
    lane = lax.broadcasted_iota(jnp.int32, (n_e, n), 1)
    assert seg_len & (seg_len - 1) == 0
    pos = lane & (seg_len - 1)

    def seg_count(mask):
        v = jnp.where(mask, 1.0, 0.0)
        parts = []
        for s in range(nseg):
            c = jnp.sum(v[:, s * seg_len:(s + 1) * seg_len], axis=1, keepdims=True)
            parts.append(jnp.broadcast_to(c, (n_e, seg_len)))
        return parts[0] if nseg == 1 else jnp.concatenate(parts, axis=1)

    def thr_step(i, thr):
        cand = thr | (jnp.int32(1) << (30 - i))
        return jnp.where(seg_count(bits >= cand) >= cap, cand, thr)

    thr = lax.fori_loop(0, 31, thr_step, jnp.zeros((n_e, n), jnp.int32))
    gt = bits > thr
    eq = bits == thr
    need = cap - seg_count(gt)

    def tie_step(i, j):
        cand = j | (jnp.int32(1) << (seg_len.bit_length() - 2 - i))
        return jnp.where(seg_count(eq & (pos < cand)) < need, cand, j)

    j = lax.fori_loop(0, seg_len.bit_length() - 1, tie_step, jnp.zeros((n_e, n), jnp.int32))
    sel = gt | (eq & (pos <= j))

    selb = jnp.where(sel, 1.0, 0.0).astype(BF)
    tiles_per_seg = seg_len // LANE
    tiles_per_chunk = MOE_TOKEN_CHUNK // LANE
    lane1 = lax.broadcasted_iota(jnp.int32, (n_e, LANE), 1)
    starts = jnp.zeros((n_e, LANE), F32)
    out = []
    run = jnp.zeros((n_e, 1), F32)
    for t in range(n // LANE):
        if t % tiles_per_seg == 0:
            run = jnp.full((n_e, 1), float((t // tiles_per_seg) * cap), F32)
        if t % tiles_per_chunk == 0:
            starts = jnp.where(lane1 == t // tiles_per_chunk, run, starts)
        inc = jnp.dot(selb[:, t * LANE:(t + 1) * LANE], tri, preferred_element_type=F32)
        out.append(inc + (run - 1.0))
        run = run + inc[:, LANE - 1:LANE]
    starts = jnp.where(lane1 == n // MOE_TOKEN_CHUNK, run, starts)
    slot = jnp.concatenate(out, axis=1).astype(jnp.int32)
    return jnp.where(sel, slot, -1), starts.astype(jnp.int32)


def _route_kernel(aff_ref, tri_ref, slot_ref, starts_ref, *, nctx, n_ctx_len, g_lo):
    n = aff_ref.shape[2]
    gb, n_e = aff_ref.shape[0], aff_ref.shape[1]
    g = pl.program_id(0) * gb + g_lo

    def run(seg_len):
        slot, starts = _route_segments(aff_ref[...].reshape(gb * n_e, n), seg_len,
                                       CAPACITY_FACTOR * seg_len // N_EXPERTS, tri_ref[...])
        slot_ref[...] = slot.reshape(gb, n_e, n)
        starts_ref[...] = starts.reshape(gb, n_e, LANE)

    if g_lo < nctx:
        pl.when(g < nctx)(functools.partial(run, n_ctx_len))
    pl.when(g >= nctx)(functools.partial(run, n))


def _route_call(aff, nctx, n_ctx_len, g_lo):
    g_n, n_e, n = aff.shape
    assert n_ctx_len % MOE_TOKEN_CHUNK == 0 and n // MOE_TOKEN_CHUNK < LANE
    gb = ROUTE_GROUPS_PER_STEP
    if (nctx - g_lo) % gb or g_n % gb:
        gb = 1
    tri = jnp.triu(jnp.ones((LANE, LANE), F32)).astype(BF)
    return pl.pallas_call(
        functools.partial(_route_kernel, nctx=nctx, n_ctx_len=n_ctx_len, g_lo=g_lo),
        grid=(g_n // gb,),
        in_specs=[pl.BlockSpec((gb, n_e, n), lambda g: (g, 0, 0)),
                  pl.BlockSpec((LANE, LANE), lambda g: (0, 0))],
        out_specs=[pl.BlockSpec((gb, n_e, n), lambda g: (g, 0, 0)),
                   pl.BlockSpec((gb, n_e, LANE), lambda g: (g, 0, 0))],
        out_shape=[jax.ShapeDtypeStruct((g_n, n_e, n), jnp.int32),
                   jax.ShapeDtypeStruct((g_n, n_e, LANE), jnp.int32)],
        compiler_params=_cparams(("parallel",)),
        name="ec_route",
    )(aff, tri)


def _gather_pairs(starts, n_slots, n_chunks):
    n_blocks = n_slots // MOE_SLOT_BLOCK
    n_pairs = n_blocks + n_chunks - 1
    assert n_blocks <= 16 and n_chunks <= 16
    s = starts[..., :n_chunks + 1]
    lo = jnp.arange(n_blocks, dtype=jnp.int32) * MOE_SLOT_BLOCK
    c_lo = jnp.sum(s[..., None, :] <= lo[:, None], axis=-1) - 1
    c_hi = jnp.sum(s[..., None, :] <= lo[:, None] + (MOE_SLOT_BLOCK - 1), axis=-1) - 1
    cnt = c_hi - c_lo + 1
    first = jnp.cumsum(cnt, axis=-1) - cnt
    j = jnp.arange(n_pairs, dtype=jnp.int32)
    r = jnp.sum(first[..., None, :] <= j[:, None], axis=-1) - 1
    pick = r[..., None] == jnp.arange(n_blocks, dtype=jnp.int32)
    c = j + jnp.sum(jnp.where(pick, (c_lo - first)[..., None, :], 0), axis=-1)
    valid = j < jnp.sum(cnt, axis=-1, keepdims=True)
    return jnp.where(valid, 256 + r * 16 + c, 0).astype(jnp.int32).reshape(-1), n_pairs


def _moe_kernel(pairs_ref, h_ref, slot_ref, gate_ref, wg_ref, wu_ref, wd_ref,
                x1_ref, g2_ref, lg_ref, lb_ref, x2_ref, xs_scr, *, n_slots, n_pairs):
    g, e = pl.program_id(0), pl.program_id(1)
    n_e = pl.num_programs(1)
    n_chunks, chunk = slot_ref.shape[1], slot_ref.shape[2]
    sb = MOE_SLOT_BLOCK

    @pl.when(e == 0)
    def _():
        x2_ref[...] = jnp.zeros_like(x2_ref)

    xs_scr[...] = jnp.zeros_like(xs_scr)
    base = (g * n_e + e) * n_pairs
    row_iota = lax.broadcasted_iota(jnp.int32, (sb, chunk), 0)
    for j in range(n_pairs):
        code = pairs_ref[base + j]
        c = code & 15
        r = (code >> 4) & 15
        rel = jnp.where(code >= 256, slot_ref[0, pl.ds(c, 1), :] - r * sb, -1)
        onehot = jnp.where(row_iota == rel, 1.0, 0.0).astype(BF)
        tok0 = pl.multiple_of(c * chunk, chunk)
        s0 = pl.multiple_of(r * sb, sb)
        xs_scr[pl.ds(s0, sb), :] += jnp.dot(onehot, h_ref[0, pl.ds(tok0, chunk), :],
                                            preferred_element_type=F32)
    xs = xs_scr[...].astype(BF)

    a = jnp.dot(xs, wg_ref[0, 0], preferred_element_type=F32)
    u = jnp.dot(xs, wu_ref[0, 0], preferred_element_type=F32)
    hh = (a * jax.nn.sigmoid(a) * u).astype(BF)
    y = jnp.dot(hh, wd_ref[0, 0], preferred_element_type=F32).astype(BF)

    slot_iota = lax.broadcasted_iota(jnp.int32, (n_slots, chunk), 0)
    for c in range(n_chunks):
        rows = slice(c * chunk, (c + 1) * chunk)
        selg = jnp.where(slot_iota == slot_ref[0, c:c + 1, :], gate_ref[0, c:c + 1, :], 0.0).astype(BF)
        x2_ref[0, rows, :] += lax.dot_general(selg, y, _TN, preferred_element_type=F32)

    @pl.when(e == n_e - 1)
    def _():
        for c in range(n_chunks):
            rows = slice(c * chunk, (c + 1) * chunk)
            x2_ref[0, rows, :] = _layer_norm(DEEPNORM_ALPHA * x1_ref[0, rows, :] + g2_ref[0] * x2_ref[0, rows, :],
                                             lg_ref[...], lb_ref[...])


def _moe_call(h2, slot, starts, aff, w_gate, w_up, w_down, n_slots,
              x1, mod, layer, nctx, rows_per_layer, ln_g, ln_b, g_lo):
    g_n, n, d = h2.shape
    _, n_e, _, ff = w_gate.shape
    n_chunks = n // MOE_TOKEN_CHUNK
    slot3 = slot.reshape(g_n * n_e, n_chunks, MOE_TOKEN_CHUNK)
    gate3 = aff.reshape(g_n * n_e, n_chunks, MOE_TOKEN_CHUNK)
    pairs, n_pairs = _gather_pairs(starts, n_slots, n_chunks)

    def mod_map(g, e, p):
        gg = g + g_lo
        return (layer * rows_per_layer + jnp.where(gg < nctx, 0, gg - nctx + 1), 0, 5)

    full = lambda shape: pl.BlockSpec(shape, lambda g, e, p: (0,) * len(shape))
    grid_spec = pltpu.PrefetchScalarGridSpec(
        num_scalar_prefetch=1,
        grid=(g_n, n_e),
        in_specs=[
            pl.BlockSpec((1, n, d), lambda g, e, p: (g, 0, 0)),
            pl.BlockSpec((1, n_chunks, MOE_TOKEN_CHUNK), lambda g, e, p: (g * n_e + e, 0, 0)),
            pl.BlockSpec((1, n_chunks, MOE_TOKEN_CHUNK), lambda g, e, p: (g * n_e + e, 0, 0)),
            pl.BlockSpec((1, 1, d, ff), lambda g, e, p: (layer, e, 0, 0)),
            pl.BlockSpec((1, 1, d, ff), lambda g, e, p: (layer, e, 0, 0)),
            pl.BlockSpec((1, 1, ff, d), lambda g, e, p: (layer, e, 0, 0)),
            pl.BlockSpec((1, n, d), lambda g, e, p: (g, 0, 0)),
            pl.BlockSpec((1, 1, d), mod_map),
            full((1, d)), full((1, d)),
        ],
        out_specs=pl.BlockSpec((1, n, d), lambda g, e, p: (g, 0, 0)),
        scratch_shapes=[pltpu.VMEM((n_slots, d), F32)],
    )
    return pl.pallas_call(
        functools.partial(_moe_kernel, n_slots=n_slots, n_pairs=n_pairs),
        grid_spec=grid_spec,
        out_shape=jax.ShapeDtypeStruct((g_n, n, d), F32),
        compiler_params=_cparams(("parallel", "arbitrary")),
        name="ec_moe",
    )(pairs, h2, slot3, gate3, w_gate, w_up, w_down, x1, mod, ln_g.reshape(1, d), ln_b.reshape(1, d))


def kernel(x, c, ctx, c_ctx, ada_w, ada_b, ln_mix_g, ln_mix_b, ln_ffn_g, ln_ffn_b, router_w, expert_w_gate, expert_w_up, expert_w_down, gqa_w_qkv, gqa_q_g, gqa_k_g, gqa_w_o, mla_w_dq, mla_q_g, mla_w_uq, mla_w_dkv, mla_kv_g, mla_w_ukv, mla_w_o):
    batch, n, d = x.shape
    n_ctx_len = ctx.shape[1]
    depth = ada_w.shape[0]
    assert n % n_ctx_len == 0 and (batch * n_ctx_len) % n == 0
    nctx = batch * n_ctx_len // n
    n_slots = CAPACITY_FACTOR * n // N_EXPERTS

    rows = -(-(batch + 1) // MOD_ROWS_PAD) * MOD_ROWS_PAD
    cvec = jnp.concatenate([c_ctx[None, :], c, jnp.zeros((rows - batch - 1, d), F32)], axis=0)
    mod = _ada_call(cvec, ada_w, ada_b).reshape(depth * rows, 1, 6 * d)

    w_gate, w_up, w_down = (w.astype(BF) for w in (expert_w_gate, expert_w_up, expert_w_down))
    xs = (ctx.reshape(nctx, n, d), x)
    for i in range(depth):
        last = i == depth - 1
        j = i // N_MIXERS
        if i % N_MIXERS == 0:
            qt, k, vt = _gqa_proj_call(xs, mod, i, nctx, rows, gqa_w_qkv[j], gqa_q_g[j], gqa_k_g[j])
            ot = _attn_call(qt, k, None, vt, nctx, n_ctx_len, GQA_HEADS, GQA_KV_HEADS, not last)
            w_o = gqa_w_o[j]
        else:
            qt, kn, kr, vt = _mla_proj_call(xs, mod, i, nctx, rows, mla_w_dq[j], mla_q_g[j], mla_w_uq[j],
                                            mla_w_dkv[j], mla_kv_g[j], mla_w_ukv[j])
            ot = _attn_call(qt, kn, kr, vt, nctx, n_ctx_len, MLA_HEADS, MLA_HEADS, not last)
            w_o = mla_w_o[j]
        g_lo = nctx if last else 0
        x1, h2, aff = _oproj_call(ot, xs, mod, i, nctx, rows, w_o, ln_mix_g[i], ln_mix_b[i],
                                  router_w[i], g_lo)
        slot, starts = _route_call(aff, nctx, n_ctx_len, g_lo)
        xs = _moe_call(h2, slot, starts, aff, w_gate, w_up, w_down, n_slots,
                       x1, mod, i, nctx, rows, ln_ffn_g[i], ln_ffn_b[i], g_lo)
    return xs
```

```python
import functools

import jax
import jax.numpy as jnp
import numpy as np
from jax import lax
from jax.experimental import pallas as pl
from jax.experimental.pallas import tpu as pltpu

D_MODEL = 1024
DEPTH = 4
GRID_W = 64
ROPE_THETA = 10000.0
NORM_EPS = 1e-6
GQA_HEADS = 8
GQA_KV_HEADS = 2
GQA_HEAD_DIM = 128
MLA_HEADS = 8
MLA_Q_RANK = 768
MLA_KV_RANK = 256
MLA_NOPE_DIM = 128
MLA_ROPE_DIM = 64
MLA_V_DIM = 128
N_EXPERTS = 16
EXPERT_FF = 1024
CAPACITY_FACTOR = 2
N_MIXERS = 2
DEEPNORM_ALPHA = (2 * DEPTH) ** 0.25
LOG2E = 1.4426950408889634

LANE = 128
MOD_ROWS_PAD = 8
ROUTE_GROUPS_PER_STEP = 4
MOE_TOKEN_CHUNK = 256
MOE_SLOT_BLOCK = 128
VMEM_LIMIT = 58 * 1024 * 1024

BF = jnp.bfloat16
F32 = jnp.float32

_NT = (((1,), (1,)), ((), ()))
_TN = (((0,), (0,)), ((), ()))


def _cparams(sem):
    return pltpu.CompilerParams(dimension_semantics=sem, vmem_limit_bytes=VMEM_LIMIT)


def _ada_kernel(c_ref, w_ref, b_ref, o_ref):
    c = c_ref[...]
    s = (c * jax.nn.sigmoid(c)).astype(BF)
    o_ref[0] = jnp.dot(s, w_ref[0].astype(BF), preferred_element_type=F32) + b_ref[0]


def _ada_call(cvec, ada_w, ada_b):
    n_layers, d, n_out = ada_w.shape
    rows = cvec.shape[0]
    tn = 1536
    return pl.pallas_call(
        _ada_kernel,
        grid=(n_layers, n_out // tn),
        in_specs=[
            pl.BlockSpec((rows, d), lambda i, j: (0, 0)),
            pl.BlockSpec((1, d, tn), lambda i, j: (i, 0, j)),
            pl.BlockSpec((1, 1, tn), lambda i, j: (i, 0, j)),
        ],
        out_specs=pl.BlockSpec((1, rows, tn), lambda i, j: (i, 0, j)),
        out_shape=jax.ShapeDtypeStruct((n_layers, rows, n_out), F32),
        compiler_params=_cparams(("parallel", "parallel")),
        name="ada_mod",
    )(cvec, ada_w, ada_b.reshape(n_layers, 1, n_out))


def _rope_tables(n, rot_dim):
    t = jnp.arange(n, dtype=jnp.int32)
    row = (t // GRID_W).astype(F32)
    col = (t % GRID_W).astype(F32)
    axis_dim = rot_dim // 2
    freqs = ROPE_THETA ** (-jnp.arange(0, axis_dim, 2, dtype=F32) / axis_dim)
    ang = jnp.concatenate([row[:, None] * freqs, col[:, None] * freqs], axis=-1)
    return jnp.cos(ang), jnp.sin(ang)


def _act_specs(x, nctx, tm, g_lo=0):
    if not isinstance(x, tuple):
        return [pl.BlockSpec((1, tm, x.shape[2]), lambda g, r: (g + g_lo, r, 0))], [x]
    xc, xl = x
    if g_lo == nctx:
        return [pl.BlockSpec((1, tm, xl.shape[2]), lambda g, r: (g, r, 0))], [xl]
    assert g_lo == 0
    last_r = xc.shape[1] // tm - 1
    spec_c = pl.BlockSpec((1, tm, xc.shape[2]),
                          lambda g, r: (jnp.minimum(g, nctx - 1), jnp.where(g < nctx, r, last_r), 0))
    spec_l = pl.BlockSpec((1, tm, xl.shape[2]),
                          lambda g, r: (jnp.maximum(g - nctx, 0), jnp.where(g < nctx, 0, r), 0))
    return [spec_c, spec_l], [xc, xl]


def _act_tile(x_refs, is_ctx):
    if len(x_refs) == 1:
        return x_refs[0][0]
    return jnp.where(is_ctx, x_refs[0][0], x_refs[1][0])


def _act_rows(x_refs, is_ctx, rows):
    if len(x_refs) == 1:
        return x_refs[0][0, rows, :]
    return jnp.where(is_ctx, x_refs[0][0, rows, :], x_refs[1][0, rows, :])


def _gqa_proj_kernel(*refs, nctx, n_x, sub):
    x_refs, refs = refs[:n_x], refs[n_x:]
    (sh_ref, sc_ref, wqv_ref, wk_ref, gq_ref, gk_ref, cs_ref, sn_ref, cst_ref, snt_ref,
     qt_ref, k_ref, vt_ref) = refs
    is_ctx = pl.program_id(0) < nctx
    hd, half = GQA_HEAD_DIM, GQA_HEAD_DIM // 2
    n_sub = k_ref.shape[1] // sub

    def project(i):
        rows = slice(i * sub, (i + 1) * sub)
        h = (_act_rows(x_refs, is_ctx, rows) * (1.0 + sc_ref[0]) + sh_ref[0]).astype(BF)
        qv = lax.dot_general(wqv_ref[...], h, _NT, preferred_element_type=F32)
        kk = jnp.dot(h, wk_ref[...], preferred_element_type=F32)
        return qv, kk

    def finish(i, qv, kk):
        rows = slice(i * sub, (i + 1) * sub)
        cost = jnp.where(is_ctx, 1.0, cst_ref[:, rows])
        sint = jnp.where(is_ctx, 0.0, snt_ref[:, rows])
        for u in range(GQA_HEADS):
            q = qv[u * hd:(u + 1) * hd]
            ms = jnp.mean(q * q, axis=0, keepdims=True)
            qn = q * lax.rsqrt(ms + NORM_EPS) * gq_ref[...]
            x0, x1 = qn[:half], qn[half:]
            qt_ref[0, u * hd:u * hd + half, rows] = (x0 * cost - x1 * sint).astype(BF)
            qt_ref[0, u * hd + half:(u + 1) * hd, rows] = (x0 * sint + x1 * cost).astype(BF)
        vt_ref[0, :, rows] = qv[GQA_HEADS * hd:].astype(BF)
        cos = jnp.where(is_ctx, 1.0, cs_ref[rows, :])
        sin = jnp.where(is_ctx, 0.0, sn_ref[rows, :])
        for j in range(GQA_KV_HEADS):
            k = kk[:, j * hd:(j + 1) * hd]
            ms = jnp.mean(k * k, axis=1, keepdims=True)
            kn = k * lax.rsqrt(ms + NORM_EPS) * gk_ref[...]
            k_ref[0, rows, j * hd:(j + 1) * hd] = (kn * cos + pltpu.roll(kn, half, 1) * sin).astype(BF)

    prev = None
    for i in range(n_sub + 1):
        cur = project(i) if i < n_sub else None
        if i >= 1:
            finish(i - 1, *prev)
        prev = cur


def _mod_spec(layer, chunk, nctx, rows_per_layer, d):
    def imap(g, *_):
        return (layer * rows_per_layer + jnp.where(g < nctx, 0, g - nctx + 1), 0, chunk)
    return pl.BlockSpec((1, 1, d), imap)


def _gqa_proj_call(x, mod, layer, nctx, rows_per_layer, w_qkv, q_g, k_g, tm=2048, sub=256):
    x_specs, x_args = _act_specs(x, nctx, tm)
    g_n = sum(a.shape[0] for a in x_args)
    _, n, d = x_args[0].shape
    hd, half = GQA_HEAD_DIM, GQA_HEAD_DIM // 2
    nq, nk = GQA_HEADS * hd, GQA_KV_HEADS * hd
    perm = np.concatenate([np.arange(0, hd, 2), np.arange(1, hd, 2)])
    wq = w_qkv[:, :nq].reshape(d, GQA_HEADS, hd)[:, :, perm].reshape(d, nq)
    wk = w_qkv[:, nq:nq + nk].reshape(d, GQA_KV_HEADS, hd)[:, :, perm].reshape(d, nk)
    wv = w_qkv[:, nq + nk:]
    wqv_t = jnp.concatenate([wq, wv], axis=1).T.astype(BF)
    wk = wk.astype(BF)
    gq = (q_g[perm] * (hd ** -0.5 * LOG2E)).reshape(hd, 1)
    gk = k_g[perm].reshape(1, hd)
    cos, sin = _rope_tables(n, hd)
    cs = jnp.concatenate([cos, cos], axis=1)
    sn = jnp.concatenate([-sin, sin], axis=1)
    kernel = functools.partial(_gqa_proj_kernel, nctx=nctx, n_x=len(x_args), sub=sub)
    full = lambda shape: pl.BlockSpec(shape, lambda g, r: (0,) * len(shape))
    return pl.pallas_call(
        kernel,
        grid=(g_n, n // tm),
        in_specs=x_specs + [
            _mod_spec(layer, 0, nctx, rows_per_layer, d),
            _mod_spec(layer, 1, nctx, rows_per_layer, d),
            full((nq + nk, d)),
            full((d, nk)),
            full((hd, 1)),
            full((1, hd)),
            pl.BlockSpec((tm, hd), lambda g, r: (r, 0)),
            pl.BlockSpec((tm, hd), lambda g, r: (r, 0)),
            pl.BlockSpec((half, tm), lambda g, r: (0, r)),
            pl.BlockSpec((half, tm), lambda g, r: (0, r)),
        ],
        out_specs=[
            pl.BlockSpec((1, nq, tm), lambda g, r: (g, 0, r)),
            pl.BlockSpec((1, tm, nk), lambda g, r: (g, r, 0)),
            pl.BlockSpec((1, nk, tm), lambda g, r: (g, 0, r)),
        ],
        out_shape=[
            jax.ShapeDtypeStruct((g_n, nq, n), BF),
            jax.ShapeDtypeStruct((g_n, n, nk), BF),
            jax.ShapeDtypeStruct((g_n, nk, n), BF),
        ],
        compiler_params=_cparams(("parallel", "parallel")),
        name="gqa_proj",
    )(*x_args, mod, mod, wqv_t, wk, gq, gk, cs, sn, cos.T, sin.T)


MLA_QK_PAD = 256


def _mla_proj_kernel(x_ref, sh_ref, sc_ref, wdq_ref, gq_ref, wuq_ref, wdkv_ref, gkv_ref,
                     wukn_ref, wuv_ref, cs_ref, sn_ref, cst_ref, snt_ref,
                     qt_ref, kn_ref, kr_ref, vt_ref, *, nctx, sub):
    is_ctx = pl.program_id(0) < nctx
    hr = MLA_ROPE_DIM // 2
    n_sub = kn_ref.shape[1] // sub

    def down(i):
        rows = slice(i * sub, (i + 1) * sub)
        h = (x_ref[0, rows, :] * (1.0 + sc_ref[0]) + sh_ref[0]).astype(BF)
        cq = jnp.dot(h, wdq_ref[...], preferred_element_type=F32)
        ckv = jnp.dot(h, wdkv_ref[...], preferred_element_type=F32)
        return cq, ckv

    def up(i, cq, ckv):
        rows = slice(i * sub, (i + 1) * sub)
        ms = jnp.mean(cq * cq, axis=1, keepdims=True)
        cqn = (cq * lax.rsqrt(ms + NORM_EPS) * gq_ref[...]).astype(BF)
        qt = lax.dot_general(wuq_ref[...], cqn, _NT, preferred_element_type=F32)
        c = ckv[:, :MLA_KV_RANK]
        ms = jnp.mean(c * c, axis=1, keepdims=True)
        cn = (c * lax.rsqrt(ms + NORM_EPS) * gkv_ref[...]).astype(BF)
        kr = ckv[:, MLA_KV_RANK:]
        cos = jnp.where(is_ctx, 1.0, cs_ref[rows, :])
        sin = jnp.where(is_ctx, 0.0, sn_ref[rows, :])
        kr_ref[0, rows, :] = (kr * cos + pltpu.roll(kr, LANE // 2, 1) * sin).astype(BF)
        kn_ref[0, rows, :] = jnp.dot(cn, wukn_ref[...], preferred_element_type=F32).astype(BF)
        vt_ref[0, :, rows] = lax.dot_general(wuv_ref[...], cn, _NT, preferred_element_type=F32).astype(BF)
        return qt

    def rope(i, qt):
        rows = slice(i * sub, (i + 1) * sub)
        qt = qt * ((MLA_NOPE_DIM + MLA_ROPE_DIM) ** -0.5 * LOG2E)
        cost = jnp.where(is_ctx, 1.0, cst_ref[:, rows])
        sint = jnp.where(is_ctx, 0.0, snt_ref[:, rows])
        for u in range(MLA_HEADS):
            src = u * (MLA_NOPE_DIM + MLA_ROPE_DIM)
            base = u * MLA_QK_PAD
            qt_ref[0, base:base + MLA_NOPE_DIM, rows] = qt[src:src + MLA_NOPE_DIM].astype(BF)
            r0 = base + MLA_NOPE_DIM
            x0 = qt[src + MLA_NOPE_DIM:src + MLA_NOPE_DIM + hr]
            x1 = qt[src + MLA_NOPE_DIM + hr:src + MLA_NOPE_DIM + 2 * hr]
            zero = jnp.zeros_like(x0).astype(BF)
            qt_ref[0, r0:r0 + hr, rows] = (x0 * cost - x1 * sint).astype(BF)
            qt_ref[0, r0 + hr:r0 + 2 * hr, rows] = zero
            qt_ref[0, r0 + 2 * hr:r0 + 3 * hr, rows] = (x0 * sint + x1 * cost).astype(BF)
            qt_ref[0, r0 + 3 * hr:r0 + 4 * hr, rows] = zero

    d_prev, q_prev = None, None
    for i in range(n_sub + 2):
        d_cur = down(i) if i < n_sub else None
        q_cur = up(i - 1, *d_prev) if 1 <= i <= n_sub else None
        if i >= 2:
            rope(i - 2, q_prev)
        d_prev, q_prev = d_cur, q_cur


def _mla_proj_call(x, mod, layer, nctx, rows_per_layer, w_dq, q_g, w_uq, w_dkv, kv_g, w_ukv,
                   tm=1024, sub=256):
    g_n, n, d = x.shape
    nh, dn, dr, dv = MLA_HEADS, MLA_NOPE_DIM, MLA_ROPE_DIM, MLA_V_DIM
    hr = dr // 2
    ev, od = np.arange(0, dr, 2), np.arange(1, dr, 2)
    wu = w_uq.reshape(MLA_Q_RANK, nh, dn + dr)
    wu = jnp.concatenate([wu[:, :, :dn], wu[:, :, dn + ev], wu[:, :, dn + od]], axis=2)
    wuq_t = wu.reshape(MLA_Q_RANK, nh * (dn + dr)).T.astype(BF)
    zk = jnp.zeros((d, hr), F32)
    wdkv = jnp.concatenate([w_dkv[:, :MLA_KV_RANK], w_dkv[:, MLA_KV_RANK + ev], zk,
                            w_dkv[:, MLA_KV_RANK + od], zk], axis=1).astype(BF)
    wkv = w_ukv.reshape(MLA_KV_RANK, nh, dn + dv)
    wukn = wkv[:, :, :dn].reshape(MLA_KV_RANK, nh * dn).astype(BF)
    wuv_t = wkv[:, :, dn:].reshape(MLA_KV_RANK, nh * dv).T.astype(BF)
    cos, sin = _rope_tables(n, dr)
    z = jnp.zeros_like(cos)
    cs = jnp.concatenate([cos, z, cos, z], axis=1)
    sn = jnp.concatenate([-sin, z, sin, z], axis=1)
    kernel = functools.partial(_mla_proj_kernel, nctx=nctx, sub=sub)
    full = lambda shape: pl.BlockSpec(shape, lambda g, r: (0,) * len(shape))
    return pl.pallas_call(
        kernel,
        grid=(g_n, n // tm),
        in_specs=[
            pl.BlockSpec((1, tm, d), lambda g, r: (g, r, 0)),
            _mod_spec(layer, 0, nctx, rows_per_layer, d),
            _mod_spec(layer, 1, nctx, rows_per_layer, d),
            full((d, MLA_Q_RANK)),
            full((1, MLA_Q_RANK)),
            full((nh * (dn + dr), MLA_Q_RANK)),
            full((d, MLA_KV_RANK + LANE)),
            full((1, MLA_KV_RANK)),
            full((MLA_KV_RANK, nh * dn)),
            full((nh * dv, MLA_KV_RANK)),
            pl.BlockSpec((tm, LANE), lambda g, r: (r, 0)),
            pl.BlockSpec((tm, LANE), lambda g, r: (r, 0)),
            pl.BlockSpec((hr, tm), lambda g, r: (0, r)),
            pl.BlockSpec((hr, tm), lambda g, r: (0, r)),
        ],
        out_specs=[
            pl.BlockSpec((1, nh * MLA_QK_PAD, tm), lambda g, r: (g, 0, r)),
            pl.BlockSpec((1, tm, nh * dn), lambda g, r: (g, r, 0)),
            pl.BlockSpec((1, tm, LANE), lambda g, r: (g, r, 0)),
            pl.BlockSpec((1, nh * dv, tm), lambda g, r: (g, 0, r)),
        ],
        out_shape=[
            jax.ShapeDtypeStruct((g_n, nh * MLA_QK_PAD, n), BF),
            jax.ShapeDtypeStruct((g_n, n, nh * dn), BF),
            jax.ShapeDtypeStruct((g_n, n, LANE), BF),
            jax.ShapeDtypeStruct((g_n, nh * dv, n), BF),
        ],
        compiler_params=_cparams(("parallel", "parallel")),
        name="mla_proj",
    )(x, mod, mod, w_dq.astype(BF), q_g.reshape(1, -1), wuq_t, wdkv, kv_g.reshape(1, -1),
      wukn, wuv_t, cs, sn, cos.T, sin.T)


SUBLANE = 8


def _attn_kernel(*refs, mla, with_lat, hb, rep, dq, dv, ck, sb):
    refs = list(refs)
    qt_ref = refs.pop(0)
    kl_ref = refs.pop(0) if with_lat else None
    kc_ref = refs.pop(0)
    krl_ref = refs.pop(0) if (mla and with_lat) else None
    krc_ref = refs.pop(0) if mla else None
    vtl_ref = refs.pop(0) if with_lat else None
    vtc_ref = refs.pop(0)
    p_scr = refs.pop()
    s_scr = refs.pop()
    ot_ref = refs.pop()

    tq = s_scr.shape[2]

    def stage_a(t, unit, ci):
        h, q0, chunks = unit
        k_ref, kr_ref, _, r0, rows, s0 = chunks[ci]
        j = h // rep
        keys = k_ref[0, r0:r0 + rows, j * LANE:(j + 1) * LANE]
        if mla:
            keys = jnp.concatenate([keys, kr_ref[0, r0:r0 + rows, :]], axis=1)
        s = jnp.dot(keys, qt_ref[0, h * dq:(h + 1) * dq, q0:q0 + tq],
                    preferred_element_type=F32)
        s_scr[t % 2, s0:s0 + rows, :] = s
        return jnp.max(s.reshape(rows // SUBLANE, SUBLANE, tq), axis=0)

    def stage_b(t, unit, ci, m):
        h, _, chunks = unit
        _, _, vt_ref, r0, rows, s0 = chunks[ci]
        j = h // rep
        l8 = None
        for r in range(0, rows, sb):
            p = jnp.exp2(s_scr[t % 2, s0 + r:s0 + r + sb, :] - m)
            ps = jnp.sum(p.reshape(sb // SUBLANE, SUBLANE, tq), axis=0)
            l8 = ps if l8 is None else l8 + ps
            p_scr[ci % 2, r:r + sb, :] = p.astype(BF)
        pv = jnp.dot(vt_ref[0, j * dv:(j + 1) * dv, r0:r0 + rows], p_scr[ci % 2, :rows, :],
                     preferred_element_type=F32)
        return l8, pv

    if with_lat:
        n_ctx = kc_ref.shape[1]
        chunks = [(kc_ref, krc_ref, vtc_ref, 0, n_ctx, 0)]
        for r0 in range(0, kl_ref.shape[1], ck):
            chunks.append((kl_ref, krl_ref, vtl_ref, r0, ck, n_ctx + r0))
        units = [(h, q0, chunks) for q0 in range(0, qt_ref.shape[2], tq) for h in range(hb)]
    else:
        units = [(h, s * tq, [(kc_ref, krc_ref, vtc_ref, s * tq, tq, 0)])
                 for s in range(kc_ref.shape[1] // tq) for h in range(hb)]

    m_cur = None
    for t in range(len(units) + 1):
        m_parts, l_parts, acc = [], [], None
        for ci in range(len(units[0][2])):
            if t < len(units):
                m_parts.append(stage_a(t, units[t], ci))
            if t >= 1:
                l8, pv = stage_b(t - 1, units[t - 1], ci, m_cur)
                l_parts.append(l8)
                acc = pv if acc is None else acc + pv
        if t >= 1:
            h, q0, _ = units[t - 1]
            l = jnp.sum(functools.reduce(jnp.add, l_parts), axis=0, keepdims=True)
            ot_ref[0, h * dv:(h + 1) * dv, q0:q0 + tq] = (acc / l).astype(BF)
        if t < len(units):
            m_cur = jnp.max(functools.reduce(jnp.maximum, m_parts), axis=0, keepdims=True)


def _attn_call(qt, k, kr, vt, nctx, n_ctx_len, n_heads, n_kv_heads, with_ctx_out,
               tq=512, tqb=1024, ck=1024, sb=128):
    g_n, _, n = qt.shape
    dq = qt.shape[1] // n_heads
    dv = vt.shape[1] // n_kv_heads
    batch = g_n - nctx
    spg = n // n_ctx_len
    rep = n_heads // n_kv_heads
    hb = n_heads
    hkv = hb // rep
    mla = kr is not None
    ot_shape = jax.ShapeDtypeStruct((g_n, n_heads * dv, n), BF)
    kern = functools.partial(_attn_kernel, mla=mla, hb=hb, rep=rep, dq=dq, dv=dv, ck=ck, sb=sb)

    lat_in = [pl.BlockSpec((1, hb * dq, tqb), lambda b, h, t: (nctx + b, h, t)),
              pl.BlockSpec((1, n, hkv * LANE), lambda b, h, t: (nctx + b, 0, h)),
              pl.BlockSpec((1, n_ctx_len, hkv * LANE), lambda b, h, t: (b // spg, b % spg, h))]
    lat_args = [qt, k, k]
    if mla:
        lat_in += [pl.BlockSpec((1, n, LANE), lambda b, h, t: (nctx + b, 0, 0)),
                   pl.BlockSpec((1, n_ctx_len, LANE), lambda b, h, t: (b // spg, b % spg, 0))]
        lat_args += [kr, kr]
    lat_in += [pl.BlockSpec((1, hkv * dv, n), lambda b, h, t: (nctx + b, h, 0)),
               pl.BlockSpec((1, hkv * dv, n_ctx_len), lambda b, h, t: (b // spg, h, b % spg))]
    lat_args += [vt, vt]
    ot = pl.pallas_call(
        functools.partial(kern, with_lat=True),
        grid=(batch, n_heads // hb, n // tqb),
        in_specs=lat_in,
        out_specs=pl.BlockSpec((1, hb * dv, tqb), lambda b, h, t: (nctx + b, h, t)),
        out_shape=ot_shape,
        scratch_shapes=[pltpu.VMEM((2, n_ctx_len + n, tq), F32),
                        pltpu.VMEM((2, max(ck, n_ctx_len), tq), BF)],
        compiler_params=_cparams(("parallel", "parallel", "parallel")),
        name="attn_latent",
    )(*lat_args)
    if not with_ctx_out:
        return ot

    ctx_in = [pl.BlockSpec((1, hb * dq, n), lambda g: (g, 0, 0)),
              pl.BlockSpec((1, n, hkv * LANE), lambda g: (g, 0, 0))]
    ctx_args = [qt, k]
    if mla:
        ctx_in += [pl.BlockSpec((1, n, LANE), lambda g: (g, 0, 0))]
        ctx_args += [kr]
    ctx_in += [pl.BlockSpec((1, hkv * dv, n), lambda g: (g, 0, 0)),
               pl.BlockSpec(memory_space=pl.ANY)]
    ctx_args += [vt, ot]
    return pl.pallas_call(
        functools.partial(kern, with_lat=False),
        grid=(nctx,),
        in_specs=ctx_in,
        out_specs=pl.BlockSpec((1, hb * dv, n), lambda g: (g, 0, 0)),
        out_shape=ot_shape,
        input_output_aliases={len(ctx_args) - 1: 0},
        scratch_shapes=[pltpu.VMEM((2, n_ctx_len, n_ctx_len), F32),
                        pltpu.VMEM((2, n_ctx_len, n_ctx_len), BF)],
        compiler_params=_cparams(("parallel",)),
        name="attn_ctx",
    )(*ctx_args)


def _layer_norm(z, g, b):
    mu = jnp.mean(z, axis=-1, keepdims=True)
    zc = z - mu
    var = jnp.mean(zc * zc, axis=-1, keepdims=True)
    return zc * lax.rsqrt(var + NORM_EPS) * g + b


def _oproj_kernel(*refs, nctx, g_lo, n_x, sub):
    x_refs, refs = refs[:n_x], refs[n_x:]
    (ot_ref, wo_ref, g1_ref, lg_ref, lb_ref, sh2_ref, sc2_ref, wr_ref,
     x1_ref, h2_ref, aff_ref) = refs
    is_ctx = pl.program_id(0) + g_lo < nctx
    tm = x1_ref.shape[1]

    def project(i):
        return lax.dot_general(ot_ref[0, :, i * sub:(i + 1) * sub], wo_ref[...], _TN,
                               preferred_element_type=F32)

    def finish(i, y):
        rows = slice(i * sub, (i + 1) * sub)
        if len(x_refs) == 1:
            x = x_refs[0][0, rows, :]
        else:
            x = jnp.where(is_ctx, x_refs[0][0, rows, :], x_refs[1][0, rows, :])
        x1 = _layer_norm(DEEPNORM_ALPHA * x + g1_ref[0] * y, lg_ref[...], lb_ref[...])
        x1_ref[0, rows, :] = x1
        h2 = x1 * (1.0 + sc2_ref[0]) + sh2_ref[0]
        hb = h2.astype(BF)
        h2_ref[0, rows, :] = hb
        lg = lax.dot_general(wr_ref[...], hb, _NT, preferred_element_type=F32)
        e = jnp.exp(lg - jnp.max(lg, axis=0, keepdims=True))
        aff_ref[0, :, rows] = e / jnp.sum(e, axis=0, keepdims=True)

    y_prev = None
    for i in range(tm // sub + 1):
        y = project(i) if i < tm // sub else None
        if i >= 1:
            finish(i - 1, y_prev)
        y_prev = y


def _oproj_call(ot, x, mod, layer, nctx, rows_per_layer, w_o, ln_g, ln_b, w_router, g_lo,
                tm=1024, sub=256):
    x_specs, x_args = _act_specs(x, nctx, tm, g_lo)
    g_n = ot.shape[0]
    _, n, d = x_args[0].shape
    n_e = w_router.shape[1]
    wr_t = w_router.T.astype(BF)
    full = lambda shape: pl.BlockSpec(shape, lambda g, r: (0,) * len(shape))

    def mspec(chunk):
        def imap(g, r):
            gg = g + g_lo
            return (layer * rows_per_layer + jnp.where(gg < nctx, 0, gg - nctx + 1), 0, chunk)
        return pl.BlockSpec((1, 1, d), imap)

    return pl.pallas_call(
        functools.partial(_oproj_kernel, nctx=nctx, g_lo=g_lo, n_x=len(x_args), sub=sub),
        grid=(g_n - g_lo, n // tm),
        in_specs=x_specs + [
            pl.BlockSpec((1, ot.shape[1], tm), lambda g, r: (g + g_lo, 0, r)),
            full(w_o.shape),
            mspec(2),
            full((1, d)), full((1, d)),
            mspec(3), mspec(4),
            full((n_e, d)),
        ],
        out_specs=[
            pl.BlockSpec((1, tm, d), lambda g, r: (g, r, 0)),
            pl.BlockSpec((1, tm, d), lambda g, r: (g, r, 0)),
            pl.BlockSpec((1, n_e, tm), lambda g, r: (g, 0, r)),
        ],
        out_shape=[
            jax.ShapeDtypeStruct((g_n - g_lo, n, d), F32),
            jax.ShapeDtypeStruct((g_n - g_lo, n, d), BF),
            jax.ShapeDtypeStruct((g_n - g_lo, n_e, n), F32),
        ],
        compiler_params=_cparams(("parallel", "parallel")),
        name="oproj_norm_router",
    )(*x_args, ot, w_o.astype(BF), mod, ln_g.reshape(1, d), ln_b.reshape(1, d), mod, mod, wr_t)


def _route_segments(aff, seg_len, cap, tri):
    n_e, n = aff.shape
    nseg = n // seg_len
    bits = pltpu.bitcast(aff, jnp.int32)
    lane = lax.broadcasted_iota(jnp.int32, (n_e, n), 1)
    assert seg_len & (seg_len - 1) == 0
    pos = lane & (seg_len - 1)

    def seg_count(mask):
        v = jnp.where(mask, 1.0, 0.0)
        parts = []
        for s in range(nseg):
            c = jnp.sum(v[:, s * seg_len:(s + 1) * seg_len], axis=1, keepdims=True)
            parts.append(jnp.broadcast_to(c, (n_e, seg_len)))
        return parts[0] if nseg == 1 else jnp.concatenate(parts, axis=1)

    def thr_step(i, thr):
        cand = thr | (jnp.int32(1) << (30 - i))
        return jnp.where(seg_count(bits >= cand) >= cap, cand, thr)

    thr = lax.fori_loop(0, 31, thr_step, jnp.zeros((n_e, n), jnp.int32))
    gt = bits > thr
    eq = bits == thr
    need = cap - seg_count(gt)

    def tie_step(i, j):
        cand = j | (jnp.int32(1) << (seg_len.bit_length() - 2 - i))
        return jnp.where(seg_count(eq & (pos < cand)) < need, cand, j)

    j = lax.fori_loop(0, seg_len.bit_length() - 1, tie_step, jnp.zeros((n_e, n), jnp.int32))
    sel = gt | (eq & (pos <= j))

    selb = jnp.where(sel, 1.0, 0.0).astype(BF)
    tiles_per_seg = seg_len // LANE
    tiles_per_chunk = MOE_TOKEN_CHUNK // LANE
    lane1 = lax.broadcasted_iota(jnp.int32, (n_e, LANE), 1)
    starts = jnp.zeros((n_e, LANE), F32)
    out = []
    run = jnp.zeros((n_e, 1), F32)
    for t in range(n // LANE):
        if t % tiles_per_seg == 0:
            run = jnp.full((n_e, 1), float((t // tiles_per_seg) * cap), F32)
        if t % tiles_per_chunk == 0:
            starts = jnp.where(lane1 == t // tiles_per_chunk, run, starts)
        inc = jnp.dot(selb[:, t * LANE:(t + 1) * LANE], tri, preferred_element_type=F32)
        out.append(inc + (run - 1.0))
        run = run + inc[:, LANE - 1:LANE]
    starts = jnp.where(lane1 == n // MOE_TOKEN_CHUNK, run, starts)
    slot = jnp.concatenate(out, axis=1).astype(jnp.int32)
    return jnp.where(sel, slot, -1), starts.astype(jnp.int32)


def _route_kernel(aff_ref, tri_ref, slot_ref, starts_ref, *, nctx, n_ctx_len, g_lo):
    n = aff_ref.shape[2]
    gb, n_e = aff_ref.shape[0], aff_ref.shape[1]
    g = pl.program_id(0) * gb + g_lo

    def run(seg_len):
        slot, starts = _route_segments(aff_ref[...].reshape(gb * n_e, n), seg_len,
                                       CAPACITY_FACTOR * seg_len // N_EXPERTS, tri_ref[...])
        slot_ref[...] = slot.reshape(gb, n_e, n)
        starts_ref[...] = starts.reshape(gb, n_e, LANE)

    if g_lo < nctx:
        pl.when(g < nctx)(functools.partial(run, n_ctx_len))
    pl.when(g >= nctx)(functools.partial(run, n))


def _route_call(aff, nctx, n_ctx_len, g_lo):
    g_n, n_e, n = aff.shape
    assert n_ctx_len % MOE_TOKEN_CHUNK == 0 and n // MOE_TOKEN_CHUNK < LANE
    gb = ROUTE_GROUPS_PER_STEP
    if (nctx - g_lo) % gb or g_n % gb:
        gb = 1
    tri = jnp.triu(jnp.ones((LANE, LANE), F32)).astype(BF)
    return pl.pallas_call(
        functools.partial(_route_kernel, nctx=nctx, n_ctx_len=n_ctx_len, g_lo=g_lo),
        grid=(g_n // gb,),
        in_specs=[pl.BlockSpec((gb, n_e, n), lambda g: (g, 0, 0)),
                  pl.BlockSpec((LANE, LANE), lambda g: (0, 0))],
        out_specs=[pl.BlockSpec((gb, n_e, n), lambda g: (g, 0, 0)),
                   pl.BlockSpec((gb, n_e, LANE), lambda g: (g, 0, 0))],
        out_shape=[jax.ShapeDtypeStruct((g_n, n_e, n), jnp.int32),
                   jax.ShapeDtypeStruct((g_n, n_e, LANE), jnp.int32)],
        compiler_params=_cparams(("parallel",)),
        name="ec_route",
    )(aff, tri)


def _gather_pairs(starts, n_slots, n_chunks):
    n_blocks = n_slots // MOE_SLOT_BLOCK
    n_pairs = n_blocks + n_chunks - 1
    assert n_blocks <= 16 and n_chunks <= 16
    s = starts[..., :n_chunks + 1]
    lo = jnp.arange(n_blocks, dtype=jnp.int32) * MOE_SLOT_BLOCK
    c_lo = jnp.sum(s[..., None, :] <= lo[:, None], axis=-1) - 1
    c_hi = jnp.sum(s[..., None, :] <= lo[:, None] + (MOE_SLOT_BLOCK - 1), axis=-1) - 1
    cnt = c_hi - c_lo + 1
    first = jnp.cumsum(cnt, axis=-1) - cnt
    j = jnp.arange(n_pairs, dtype=jnp.int32)
    r = jnp.sum(first[..., None, :] <= j[:, None], axis=-1) - 1
    pick = r[..., None] == jnp.arange(n_blocks, dtype=jnp.int32)
    c = j + jnp.sum(jnp.where(pick, (c_lo - first)[..., None, :], 0), axis=-1)
    valid = j < jnp.sum(cnt, axis=-1, keepdims=True)
    return jnp.where(valid, 256 + r * 16 + c, 0).astype(jnp.int32).reshape(-1), n_pairs


def _moe_kernel(pairs_ref, h_ref, slot_ref, gate_ref, wg_ref, wu_ref, wd_ref,
                x1_ref, g2_ref, lg_ref, lb_ref, x2_ref, xs_scr, *, n_slots, n_pairs):
    g, e = pl.program_id(0), pl.program_id(1)
    n_e = pl.num_programs(1)
    n_chunks, chunk = slot_ref.shape[1], slot_ref.shape[2]
    sb = MOE_SLOT_BLOCK

    @pl.when(e == 0)
    def _():
        x2_ref[...] = jnp.zeros_like(x2_ref)

    xs_scr[...] = jnp.zeros_like(xs_scr)
    base = (g * n_e + e) * n_pairs
    row_iota = lax.broadcasted_iota(jnp.int32, (sb, chunk), 0)
    for j in range(n_pairs):
        code = pairs_ref[base + j]
        c = code & 15
        r = (code >> 4) & 15
        rel = jnp.where(code >= 256, slot_ref[0, pl.ds(c, 1), :] - r * sb, -1)
        onehot = jnp.where(row_iota == rel, 1.0, 0.0).astype(BF)
        tok0 = pl.multiple_of(c * chunk, chunk)
        s0 = pl.multiple_of(r * sb, sb)
        xs_scr[pl.ds(s0, sb), :] += jnp.dot(onehot, h_ref[0, pl.ds(tok0, chunk), :],
                                            preferred_element_type=F32)
    xs = xs_scr[...].astype(BF)

    a = jnp.dot(xs, wg_ref[0, 0], preferred_element_type=F32)
    u = jnp.dot(xs, wu_ref[0, 0], preferred_element_type=F32)
    hh = (a * jax.nn.sigmoid(a) * u).astype(BF)
    y = jnp.dot(hh, wd_ref[0, 0], preferred_element_type=F32).astype(BF)

    slot_iota = lax.broadcasted_iota(jnp.int32, (n_slots, chunk), 0)
    for c in range(n_chunks):
        rows = slice(c * chunk, (c + 1) * chunk)
        selg = jnp.where(slot_iota == slot_ref[0, c:c + 1, :], gate_ref[0, c:c + 1, :], 0.0).astype(BF)
        x2_ref[0, rows, :] += lax.dot_general(selg, y, _TN, preferred_element_type=F32)

    @pl.when(e == n_e - 1)
    def _():
        for c in range(n_chunks):
            rows = slice(c * chunk, (c + 1) * chunk)
            x2_ref[0, rows, :] = _layer_norm(DEEPNORM_ALPHA * x1_ref[0, rows, :] + g2_ref[0] * x2_ref[0, rows, :],
                                             lg_ref[...], lb_ref[...])


def _moe_call(h2, slot, starts, aff, w_gate, w_up, w_down, n_slots,
              x1, mod, layer, nctx, rows_per_layer, ln_g, ln_b, g_lo):
    g_n, n, d = h2.shape
    _, n_e, _, ff = w_gate.shape
    n_chunks = n // MOE_TOKEN_CHUNK
    slot3 = slot.reshape(g_n * n_e, n_chunks, MOE_TOKEN_CHUNK)
    gate3 = aff.reshape(g_n * n_e, n_chunks, MOE_TOKEN_CHUNK)
    pairs, n_pairs = _gather_pairs(starts, n_slots, n_chunks)

    def mod_map(g, e, p):
        gg = g + g_lo
        return (layer * rows_per_layer + jnp.where(gg < nctx, 0, gg - nctx + 1), 0, 5)

    full = lambda shape: pl.BlockSpec(shape, lambda g, e, p: (0,) * len(shape))
    grid_spec = pltpu.PrefetchScalarGridSpec(
        num_scalar_prefetch=1,
        grid=(g_n, n_e),
        in_specs=[
            pl.BlockSpec((1, n, d), lambda g, e, p: (g, 0, 0)),
            pl.BlockSpec((1, n_chunks, MOE_TOKEN_CHUNK), lambda g, e, p: (g * n_e + e, 0, 0)),
            pl.BlockSpec((1, n_chunks, MOE_TOKEN_CHUNK), lambda g, e, p: (g * n_e + e, 0, 0)),
            pl.BlockSpec((1, 1, d, ff), lambda g, e, p: (layer, e, 0, 0)),
            pl.BlockSpec((1, 1, d, ff), lambda g, e, p: (layer, e, 0, 0)),
            pl.BlockSpec((1, 1, ff, d), lambda g, e, p: (layer, e, 0, 0)),
            pl.BlockSpec((1, n, d), lambda g, e, p: (g, 0, 0)),
            pl.BlockSpec((1, 1, d), mod_map),
            full((1, d)), full((1, d)),
        ],
        out_specs=pl.BlockSpec((1, n, d), lambda g, e, p: (g, 0, 0)),
        scratch_shapes=[pltpu.VMEM((n_slots, d), F32)],
    )
    return pl.pallas_call(
        functools.partial(_moe_kernel, n_slots=n_slots, n_pairs=n_pairs),
        grid_spec=grid_spec,
        out_shape=jax.ShapeDtypeStruct((g_n, n, d), F32),
        compiler_params=_cparams(("parallel", "arbitrary")),
        name="ec_moe",
    )(pairs, h2, slot3, gate3, w_gate, w_up, w_down, x1, mod, ln_g.reshape(1, d), ln_b.reshape(1, d))


def kernel(x, c, ctx, c_ctx, ada_w, ada_b, ln_mix_g, ln_mix_b, ln_ffn_g, ln_ffn_b, router_w, expert_w_gate, expert_w_up, expert_w_down, gqa_w_qkv, gqa_q_g, gqa_k_g, gqa_w_o, mla_w_dq, mla_q_g, mla_w_uq, mla_w_dkv, mla_kv_g, mla_w_ukv, mla_w_o):
    batch, n, d = x.shape
    n_ctx_len = ctx.shape[1]
    depth = ada_w.shape[0]
    assert n % n_ctx_len == 0 and (batch * n_ctx_len) % n == 0
    nctx = batch * n_ctx_len // n
    n_slots = CAPACITY_FACTOR * n // N_EXPERTS

    rows = -(-(batch + 1) // MOD_ROWS_PAD) * MOD_ROWS_PAD
    cvec = jnp.concatenate([c_ctx[None, :], c, jnp.zeros((rows - batch - 1, d), F32)], axis=0)
    mod = _ada_call(cvec, ada_w, ada_b).reshape(depth * rows, 1, 6 * d)

    w_gate, w_up, w_down = (w.astype(BF) for w in (expert_w_gate, expert_w_up, expert_w_down))
    xs = (ctx.reshape(nctx, n, d), x)
    for i in range(depth):
        last = i == depth - 1
        j = i // N_MIXERS
        if i % N_MIXERS == 0:
            qt, k, vt = _gqa_proj_call(xs, mod, i, nctx, rows, gqa_w_qkv[j], gqa_q_g[j], gqa_k_g[j])
            ot = _attn_call(qt, k, None, vt, nctx, n_ctx_len, GQA_HEADS, GQA_KV_HEADS, not last)
            w_o = gqa_w_o[j]
        else:
            qt, kn, kr, vt = _mla_proj_call(xs, mod, i, nctx, rows, mla_w_dq[j], mla_q_g[j], mla_w_uq[j],
                                            mla_w_dkv[j], mla_kv_g[j], mla_w_ukv[j])
            ot = _attn_call(qt, kn, kr, vt, nctx, n_ctx_len, MLA_HEADS, MLA_HEADS, not last)
            w_o = mla_w_o[j]
        g_lo = nctx if last else 0
        x1, h2, aff = _oproj_call(ot, xs, mod, i, nctx, rows, w_o, ln_mix_g[i], ln_mix_b[i],
                                  router_w[i], g_lo)
        slot, starts = _route_call(aff, nctx, n_ctx_len, g_lo)
        xs = _moe_call(h2, slot, starts, aff, w_gate, w_up, w_down, n_slots,
                       x1, mod, i, nctx, rows, ln_ffn_g[i], ln_ffn_b[i], g_lo)
    return xs
```

```python
import functools

import jax
import jax.numpy as jnp
import numpy as np
from jax import lax
from jax.experimental import pallas as pl
from jax.experimental.pallas import tpu as pltpu

D_MODEL = 1024
DEPTH = 4
GRID_W = 64
ROPE_THETA = 10000.0
NORM_EPS = 1e-6
GQA_HEADS = 8
GQA_KV_HEADS = 2
GQA_HEAD_DIM = 128
MLA_HEADS = 8
MLA_Q_RANK = 768
MLA_KV_RANK = 256
MLA_NOPE_DIM = 128
MLA_ROPE_DIM = 64
MLA_V_DIM = 128
N_EXPERTS = 16
EXPERT_FF = 1024
CAPACITY_FACTOR = 2
N_MIXERS = 2
DEEPNORM_ALPHA = (2 * DEPTH) ** 0.25
LOG2E = 1.4426950408889634

LANE = 128
SUBLANE = 8
MOD_ROWS_PAD = 8
VMEM_LIMIT = 58 * 1024 * 1024

ADA_COLS = 1536
GQA_PROJ_ROWS, GQA_PROJ_SUB = 2048, 256
MLA_PROJ_ROWS, MLA_PROJ_SUB = 1024, 256
OPROJ_ROWS, OPROJ_SUB = 1024, 256
ATTN_Q_STEP, ATTN_Q_UNIT = 1024, 512
ATTN_KEY_CHUNK = 1024
ATTN_EXP_ROWS = 128
ROUTE_GROUPS_PER_STEP = 4
MOE_TOKEN_CHUNK = 256
MOE_SLOT_BLOCK = 128

BF = jnp.bfloat16
F32 = jnp.float32

_NT = (((1,), (1,)), ((), ()))
_TN = (((0,), (0,)), ((), ()))


def _cparams(sem):
    return pltpu.CompilerParams(dimension_semantics=sem, vmem_limit_bytes=VMEM_LIMIT)


def _ada_kernel(c_ref, w_ref, b_ref, o_ref):
    c = c_ref[...]
    s = (c * jax.nn.sigmoid(c)).astype(BF)
    o_ref[0] = jnp.dot(s, w_ref[0].astype(BF), preferred_element_type=F32) + b_ref[0]


def _ada_call(cvec, ada_w, ada_b):
    n_layers, d, n_out = ada_w.shape
    rows = cvec.shape[0]
    tn = ADA_COLS
    return pl.pallas_call(
        _ada_kernel,
        grid=(n_layers, n_out // tn),
        in_specs=[
            pl.BlockSpec((rows, d), lambda i, j: (0, 0)),
            pl.BlockSpec((1, d, tn), lambda i, j: (i, 0, j)),
            pl.BlockSpec((1, 1, tn), lambda i, j: (i, 0, j)),
        ],
        out_specs=pl.BlockSpec((1, rows, tn), lambda i, j: (i, 0, j)),
        out_shape=jax.ShapeDtypeStruct((n_layers, rows, n_out), F32),
        compiler_params=_cparams(("parallel", "parallel")),
        name="ada_mod",
    )(cvec, ada_w, ada_b.reshape(n_layers, 1, n_out))


def _rope_tables(n, rot_dim):
    t = jnp.arange(n, dtype=jnp.int32)
    row = (t // GRID_W).astype(F32)
    col = (t % GRID_W).astype(F32)
    axis_dim = rot_dim // 2
    freqs = ROPE_THETA ** (-jnp.arange(0, axis_dim, 2, dtype=F32) / axis_dim)
    ang = jnp.concatenate([row[:, None] * freqs, col[:, None] * freqs], axis=-1)
    return jnp.cos(ang), jnp.sin(ang)


def _act_specs(x, nctx, tm, g_lo=0):
    if not isinstance(x, tuple):
        return [pl.BlockSpec((1, tm, x.shape[2]), lambda g, r: (g + g_lo, r, 0))], [x]
    xc, xl = x
    if g_lo == nctx:
        return [pl.BlockSpec((1, tm, xl.shape[2]), lambda g, r: (g, r, 0))], [xl]
    assert g_lo == 0
    last_r = xc.shape[1] // tm - 1
    spec_c = pl.BlockSpec((1, tm, xc.shape[2]),
                          lambda g, r: (jnp.minimum(g, nctx - 1), jnp.where(g < nctx, r, last_r), 0))
    spec_l = pl.BlockSpec((1, tm, xl.shape[2]),
                          lambda g, r: (jnp.maximum(g - nctx, 0), jnp.where(g < nctx, 0, r), 0))
    return [spec_c, spec_l], [xc, xl]


def _act_rows(x_refs, is_ctx, rows):
    if len(x_refs) == 1:
        return x_refs[0][0, rows, :]
    return jnp.where(is_ctx, x_refs[0][0, rows, :], x_refs[1][0, rows, :])


def _gqa_proj_kernel(*refs, nctx, n_x, sub):
    x_refs, refs = refs[:n_x], refs[n_x:]
    (sh_ref, sc_ref, wqv_ref, wk_ref, gq_ref, gk_ref, cs_ref, sn_ref, cst_ref, snt_ref,
     qt_ref, k_ref, vt_ref) = refs
    is_ctx = pl.program_id(0) < nctx
    hd, half = GQA_HEAD_DIM, GQA_HEAD_DIM // 2
    n_sub = k_ref.shape[1] // sub

    def project(i):
        rows = slice(i * sub, (i + 1) * sub)
        h = (_act_rows(x_refs, is_ctx, rows) * (1.0 + sc_ref[0]) + sh_ref[0]).astype(BF)
        qv = lax.dot_general(wqv_ref[...], h, _NT, preferred_element_type=F32)
        kk = jnp.dot(h, wk_ref[...], preferred_element_type=F32)
        return qv, kk

    def finish(i, qv, kk):
        rows = slice(i * sub, (i + 1) * sub)
        cost = jnp.where(is_ctx, 1.0, cst_ref[:, rows])
        sint = jnp.where(is_ctx, 0.0, snt_ref[:, rows])
        for u in range(GQA_HEADS):
            q = qv[u * hd:(u + 1) * hd]
            ms = jnp.mean(q * q, axis=0, keepdims=True)
            qn = q * lax.rsqrt(ms + NORM_EPS) * gq_ref[...]
            x0, x1 = qn[:half], qn[half:]
            qt_ref[0, u * hd:u * hd + half, rows] = (x0 * cost - x1 * sint).astype(BF)
            qt_ref[0, u * hd + half:(u + 1) * hd, rows] = (x0 * sint + x1 * cost).astype(BF)
        vt_ref[0, :, rows] = qv[GQA_HEADS * hd:].astype(BF)
        cos = jnp.where(is_ctx, 1.0, cs_ref[rows, :])
        sin = jnp.where(is_ctx, 0.0, sn_ref[rows, :])
        for j in range(GQA_KV_HEADS):
            k = kk[:, j * hd:(j + 1) * hd]
            ms = jnp.mean(k * k, axis=1, keepdims=True)
            kn = k * lax.rsqrt(ms + NORM_EPS) * gk_ref[...]
            k_ref[0, rows, j * hd:(j + 1) * hd] = (kn * cos + pltpu.roll(kn, half, 1) * sin).astype(BF)

    prev = None
    for i in range(n_sub + 1):
        cur = project(i) if i < n_sub else None
        if i >= 1:
            finish(i - 1, *prev)
        prev = cur


def _mod_spec(layer, chunk, nctx, rows_per_layer, d):
    def imap(g, *_):
        return (layer * rows_per_layer + jnp.where(g < nctx, 0, g - nctx + 1), 0, chunk)
    return pl.BlockSpec((1, 1, d), imap)


def _gqa_proj_call(x, mod, layer, nctx, rows_per_layer, w_qkv, q_g, k_g,
                   tm=GQA_PROJ_ROWS, sub=GQA_PROJ_SUB):
    x_specs, x_args = _act_specs(x, nctx, tm)
    g_n = sum(a.shape[0] for a in x_args)
    _, n, d = x_args[0].shape
    hd, half = GQA_HEAD_DIM, GQA_HEAD_DIM // 2
    nq, nk = GQA_HEADS * hd, GQA_KV_HEADS * hd
    perm = np.concatenate([np.arange(0, hd, 2), np.arange(1, hd, 2)])
    wq = w_qkv[:, :nq].reshape(d, GQA_HEADS, hd)[:, :, perm].reshape(d, nq)
    wk = w_qkv[:, nq:nq + nk].reshape(d, GQA_KV_HEADS, hd)[:, :, perm].reshape(d, nk)
    wv = w_qkv[:, nq + nk:]
    wqv_t = jnp.concatenate([wq, wv], axis=1).T.astype(BF)
    wk = wk.astype(BF)
    gq = (q_g[perm] * (hd ** -0.5 * LOG2E)).reshape(hd, 1)
    gk = k_g[perm].reshape(1, hd)
    cos, sin = _rope_tables(n, hd)
    cs = jnp.concatenate([cos, cos], axis=1)
    sn = jnp.concatenate([-sin, sin], axis=1)
    kernel = functools.partial(_gqa_proj_kernel, nctx=nctx, n_x=len(x_args), sub=sub)
    full = lambda shape: pl.BlockSpec(shape, lambda g, r: (0,) * len(shape))
    return pl.pallas_call(
        kernel,
        grid=(g_n, n // tm),
        in_specs=x_specs + [
            _mod_spec(layer, 0, nctx, rows_per_layer, d),
            _mod_spec(layer, 1, nctx, rows_per_layer, d),
            full((nq + nk, d)),
            full((d, nk)),
            full((hd, 1)),
            full((1, hd)),
            pl.BlockSpec((tm, hd), lambda g, r: (r, 0)),
            pl.BlockSpec((tm, hd), lambda g, r: (r, 0)),
            pl.BlockSpec((half, tm), lambda g, r: (0, r)),
            pl.BlockSpec((half, tm), lambda g, r: (0, r)),
        ],
        out_specs=[
            pl.BlockSpec((1, nq, tm), lambda g, r: (g, 0, r)),
            pl.BlockSpec((1, tm, nk), lambda g, r: (g, r, 0)),
            pl.BlockSpec((1, nk, tm), lambda g, r: (g, 0, r)),
        ],
        out_shape=[
            jax.ShapeDtypeStruct((g_n, nq, n), BF),
            jax.ShapeDtypeStruct((g_n, n, nk), BF),
            jax.ShapeDtypeStruct((g_n, nk, n), BF),
        ],
        compiler_params=_cparams(("parallel", "parallel")),
        name="gqa_proj",
    )(*x_args, mod, mod, wqv_t, wk, gq, gk, cs, sn, cos.T, sin.T)


MLA_QK_PAD = 256


def _mla_proj_kernel(x_ref, sh_ref, sc_ref, wdq_ref, gq_ref, wuq_ref, wdkv_ref, gkv_ref,
                     wukn_ref, wuv_ref, cs_ref, sn_ref, cst_ref, snt_ref,
                     qt_ref, kn_ref, kr_ref, vt_ref, *, nctx, sub):
    is_ctx = pl.program_id(0) < nctx
    hr = MLA_ROPE_DIM // 2
    n_sub = kn_ref.shape[1] // sub

    def down(i):
        rows = slice(i * sub, (i + 1) * sub)
        h = (x_ref[0, rows, :] * (1.0 + sc_ref[0]) + sh_ref[0]).astype(BF)
        cq = jnp.dot(h, wdq_ref[...], preferred_element_type=F32)
        ckv = jnp.dot(h, wdkv_ref[...], preferred_element_type=F32)
        return cq, ckv

    def up(i, cq, ckv):
        rows = slice(i * sub, (i + 1) * sub)
        ms = jnp.mean(cq * cq, axis=1, keepdims=True)
        cqn = (cq * lax.rsqrt(ms + NORM_EPS) * gq_ref[...]).astype(BF)
        qt = lax.dot_general(wuq_ref[...], cqn, _NT, preferred_element_type=F32)
        c = ckv[:, :MLA_KV_RANK]
        ms = jnp.mean(c * c, axis=1, keepdims=True)
        cn = (c * lax.rsqrt(ms + NORM_EPS) * gkv_ref[...]).astype(BF)
        kr = ckv[:, MLA_KV_RANK:]
        cos = jnp.where(is_ctx, 1.0, cs_ref[rows, :])
        sin = jnp.where(is_ctx, 0.0, sn_ref[rows, :])
        kr_ref[0, rows, :] = (kr * cos + pltpu.roll(kr, LANE // 2, 1) * sin).astype(BF)
        kn_ref[0, rows, :] = jnp.dot(cn, wukn_ref[...], preferred_element_type=F32).astype(BF)
        vt_ref[0, :, rows] = lax.dot_general(wuv_ref[...], cn, _NT, preferred_element_type=F32).astype(BF)
        return qt

    def rope(i, qt):
        rows = slice(i * sub, (i + 1) * sub)
        qt = qt * ((MLA_NOPE_DIM + MLA_ROPE_DIM) ** -0.5 * LOG2E)
        cost = jnp.where(is_ctx, 1.0, cst_ref[:, rows])
        sint = jnp.where(is_ctx, 0.0, snt_ref[:, rows])
        for u in range(MLA_HEADS):
            src = u * (MLA_NOPE_DIM + MLA_ROPE_DIM)
            base = u * MLA_QK_PAD
            qt_ref[0, base:base + MLA_NOPE_DIM, rows] = qt[src:src + MLA_NOPE_DIM].astype(BF)
            r0 = base + MLA_NOPE_DIM
            x0 = qt[src + MLA_NOPE_DIM:src + MLA_NOPE_DIM + hr]
            x1 = qt[src + MLA_NOPE_DIM + hr:src + MLA_NOPE_DIM + 2 * hr]
            zero = jnp.zeros_like(x0).astype(BF)
            qt_ref[0, r0:r0 + hr, rows] = (x0 * cost - x1 * sint).astype(BF)
            qt_ref[0, r0 + hr:r0 + 2 * hr, rows] = zero
            qt_ref[0, r0 + 2 * hr:r0 + 3 * hr, rows] = (x0 * sint + x1 * cost).astype(BF)
            qt_ref[0, r0 + 3 * hr:r0 + 4 * hr, rows] = zero

    d_prev, q_prev = None, None
    for i in range(n_sub + 2):
        d_cur = down(i) if i < n_sub else None
        q_cur = up(i - 1, *d_prev) if 1 <= i <= n_sub else None
        if i >= 2:
            rope(i - 2, q_prev)
        d_prev, q_prev = d_cur, q_cur


def _mla_proj_call(x, mod, layer, nctx, rows_per_layer, w_dq, q_g, w_uq, w_dkv, kv_g, w_ukv,
                   tm=MLA_PROJ_ROWS, sub=MLA_PROJ_SUB):
    g_n, n, d = x.shape
    nh, dn, dr, dv = MLA_HEADS, MLA_NOPE_DIM, MLA_ROPE_DIM, MLA_V_DIM
    hr = dr // 2
    ev, od = np.arange(0, dr, 2), np.arange(1, dr, 2)
    wu = w_uq.reshape(MLA_Q_RANK, nh, dn + dr)
    wu = jnp.concatenate([wu[:, :, :dn], wu[:, :, dn + ev], wu[:, :, dn + od]], axis=2)
    wuq_t = wu.reshape(MLA_Q_RANK, nh * (dn + dr)).T.astype(BF)
    zk = jnp.zeros((d, hr), F32)
    wdkv = jnp.concatenate([w_dkv[:, :MLA_KV_RANK], w_dkv[:, MLA_KV_RANK + ev], zk,
                            w_dkv[:, MLA_KV_RANK + od], zk], axis=1).astype(BF)
    wkv = w_ukv.reshape(MLA_KV_RANK, nh, dn + dv)
    wukn = wkv[:, :, :dn].reshape(MLA_KV_RANK, nh * dn).astype(BF)
    wuv_t = wkv[:, :, dn:].reshape(MLA_KV_RANK, nh * dv).T.astype(BF)
    cos, sin = _rope_tables(n, dr)
    z = jnp.zeros_like(cos)
    cs = jnp.concatenate([cos, z, cos, z], axis=1)
    sn = jnp.concatenate([-sin, z, sin, z], axis=1)
    kernel = functools.partial(_mla_proj_kernel, nctx=nctx, sub=sub)
    full = lambda shape: pl.BlockSpec(shape, lambda g, r: (0,) * len(shape))
    return pl.pallas_call(
        kernel,
        grid=(g_n, n // tm),
        in_specs=[
            pl.BlockSpec((1, tm, d), lambda g, r: (g, r, 0)),
            _mod_spec(layer, 0, nctx, rows_per_layer, d),
            _mod_spec(layer, 1, nctx, rows_per_layer, d),
            full((d, MLA_Q_RANK)),
            full((1, MLA_Q_RANK)),
            full((nh * (dn + dr), MLA_Q_RANK)),
            full((d, MLA_KV_RANK + LANE)),
            full((1, MLA_KV_RANK)),
            full((MLA_KV_RANK, nh * dn)),
            full((nh * dv, MLA_KV_RANK)),
            pl.BlockSpec((tm, LANE), lambda g, r: (r, 0)),
            pl.BlockSpec((tm, LANE), lambda g, r: (r, 0)),
            pl.BlockSpec((hr, tm), lambda g, r: (0, r)),
            pl.BlockSpec((hr, tm), lambda g, r: (0, r)),
        ],
        out_specs=[
            pl.BlockSpec((1, nh * MLA_QK_PAD, tm), lambda g, r: (g, 0, r)),
            pl.BlockSpec((1, tm, nh * dn), lambda g, r: (g, r, 0)),
            pl.BlockSpec((1, tm, LANE), lambda g, r: (g, r, 0)),
            pl.BlockSpec((1, nh * dv, tm), lambda g, r: (g, 0, r)),
        ],
        out_shape=[
            jax.ShapeDtypeStruct((g_n, nh * MLA_QK_PAD, n), BF),
            jax.ShapeDtypeStruct((g_n, n, nh * dn), BF),
            jax.ShapeDtypeStruct((g_n, n, LANE), BF),
            jax.ShapeDtypeStruct((g_n, nh * dv, n), BF),
        ],
        compiler_params=_cparams(("parallel", "parallel")),
        name="mla_proj",
    )(x, mod, mod, w_dq.astype(BF), q_g.reshape(1, -1), wuq_t, wdkv, kv_g.reshape(1, -1),
      wukn, wuv_t, cs, sn, cos.T, sin.T)


def _attn_kernel(*refs, mla, with_lat, hb, rep, dq, dv, ck, sb):
    refs = list(refs)
    qt_ref = refs.pop(0)
    kl_ref = refs.pop(0) if with_lat else None
    kc_ref = refs.pop(0)
    krl_ref = refs.pop(0) if (mla and with_lat) else None
    krc_ref = refs.pop(0) if mla else None
    vtl_ref = refs.pop(0) if with_lat else None
    vtc_ref = refs.pop(0)
    p_scr = refs.pop()
    s_scr = refs.pop()
    ot_ref = refs.pop()

    tq = s_scr.shape[2]

    def stage_a(t, unit, ci):
        h, q0, chunks = unit
        k_ref, kr_ref, _, r0, rows, s0 = chunks[ci]
        j = h // rep
        keys = k_ref[0, r0:r0 + rows, j * LANE:(j + 1) * LANE]
        if mla:
            keys = jnp.concatenate([keys, kr_ref[0, r0:r0 + rows, :]], axis=1)
        s = jnp.dot(keys, qt_ref[0, h * dq:(h + 1) * dq, q0:q0 + tq],
                    preferred_element_type=F32)
        s_scr[t % 2, s0:s0 + rows, :] = s
        return jnp.max(s.reshape(rows // SUBLANE, SUBLANE, tq), axis=0)

    def stage_b(t, unit, ci, m):
        h, _, chunks = unit
        _, _, vt_ref, r0, rows, s0 = chunks[ci]
        j = h // rep
        l8 = None
        for r in range(0, rows, sb):
            p = jnp.exp2(s_scr[t % 2, s0 + r:s0 + r + sb, :] - m)
            ps = jnp.sum(p.reshape(sb // SUBLANE, SUBLANE, tq), axis=0)
            l8 = ps if l8 is None else l8 + ps
            p_scr[ci % 2, r:r + sb, :] = p.astype(BF)
        pv = jnp.dot(vt_ref[0, j * dv:(j + 1) * dv, r0:r0 + rows], p_scr[ci % 2, :rows, :],
                     preferred_element_type=F32)
        return l8, pv

    if with_lat:
        n_ctx = kc_ref.shape[1]
        chunks = [(kc_ref, krc_ref, vtc_ref, 0, n_ctx, 0)]
        for r0 in range(0, kl_ref.shape[1], ck):
            chunks.append((kl_ref, krl_ref, vtl_ref, r0, ck, n_ctx + r0))
        units = [(h, q0, chunks) for q0 in range(0, qt_ref.shape[2], tq) for h in range(hb)]
    else:
        units = [(h, s * tq, [(kc_ref, krc_ref, vtc_ref, s * tq, tq, 0)])
                 for s in range(kc_ref.shape[1] // tq) for h in range(hb)]

    m_cur = None
    for t in range(len(units) + 1):
        m_parts, l_parts, acc = [], [], None
        for ci in range(len(units[0][2])):
            if t < len(units):
                m_parts.append(stage_a(t, units[t], ci))
            if t >= 1:
                l8, pv = stage_b(t - 1, units[t - 1], ci, m_cur)
                l_parts.append(l8)
                acc = pv if acc is None else acc + pv
        if t >= 1:
            h, q0, _ = units[t - 1]
            l = jnp.sum(functools.reduce(jnp.add, l_parts), axis=0, keepdims=True)
            ot_ref[0, h * dv:(h + 1) * dv, q0:q0 + tq] = (acc / l).astype(BF)
        if t < len(units):
            m_cur = jnp.max(functools.reduce(jnp.maximum, m_parts), axis=0, keepdims=True)


def _attn_call(qt, k, kr, vt, nctx, n_ctx_len, n_heads, n_kv_heads, with_ctx_out,
               tq=ATTN_Q_UNIT, tqb=ATTN_Q_STEP, ck=ATTN_KEY_CHUNK, sb=ATTN_EXP_ROWS):
    g_n, _, n = qt.shape
    dq = qt.shape[1] // n_heads
    dv = vt.shape[1] // n_kv_heads
    batch = g_n - nctx
    spg = n // n_ctx_len
    rep = n_heads // n_kv_heads
    hb = n_heads
    hkv = hb // rep
    mla = kr is not None
    ot_shape = jax.ShapeDtypeStruct((g_n, n_heads * dv, n), BF)
    kern = functools.partial(_attn_kernel, mla=mla, hb=hb, rep=rep, dq=dq, dv=dv, ck=ck, sb=sb)

    lat_in = [pl.BlockSpec((1, hb * dq, tqb), lambda b, h, t: (nctx + b, h, t)),
              pl.BlockSpec((1, n, hkv * LANE), lambda b, h, t: (nctx + b, 0, h)),
              pl.BlockSpec((1, n_ctx_len, hkv * LANE), lambda b, h, t: (b // spg, b % spg, h))]
    lat_args = [qt, k, k]
    if mla:
        lat_in += [pl.BlockSpec((1, n, LANE), lambda b, h, t: (nctx + b, 0, 0)),
                   pl.BlockSpec((1, n_ctx_len, LANE), lambda b, h, t: (b // spg, b % spg, 0))]
        lat_args += [kr, kr]
    lat_in += [pl.BlockSpec((1, hkv * dv, n), lambda b, h, t: (nctx + b, h, 0)),
               pl.BlockSpec((1, hkv * dv, n_ctx_len), lambda b, h, t: (b // spg, h, b % spg))]
    lat_args += [vt, vt]
    ot = pl.pallas_call(
        functools.partial(kern, with_lat=True),
        grid=(batch, n_heads // hb, n // tqb),
        in_specs=lat_in,
        out_specs=pl.BlockSpec((1, hb * dv, tqb), lambda b, h, t: (nctx + b, h, t)),
        out_shape=ot_shape,
        scratch_shapes=[pltpu.VMEM((2, n_ctx_len + n, tq), F32),
                        pltpu.VMEM((2, max(ck, n_ctx_len), tq), BF)],
        compiler_params=_cparams(("parallel", "parallel", "parallel")),
        name="attn_latent",
    )(*lat_args)
    if not with_ctx_out:
        return ot

    ctx_in = [pl.BlockSpec((1, hb * dq, n), lambda g: (g, 0, 0)),
              pl.BlockSpec((1, n, hkv * LANE), lambda g: (g, 0, 0))]
    ctx_args = [qt, k]
    if mla:
        ctx_in += [pl.BlockSpec((1, n, LANE), lambda g: (g, 0, 0))]
        ctx_args += [kr]
    ctx_in += [pl.BlockSpec((1, hkv * dv, n), lambda g: (g, 0, 0)),
               pl.BlockSpec(memory_space=pl.ANY)]
    ctx_args += [vt, ot]
    return pl.pallas_call(
        functools.partial(kern, with_lat=False),
        grid=(nctx,),
        in_specs=ctx_in,
        out_specs=pl.BlockSpec((1, hb * dv, n), lambda g: (g, 0, 0)),
        out_shape=ot_shape,
        input_output_aliases={len(ctx_args) - 1: 0},
        scratch_shapes=[pltpu.VMEM((2, n_ctx_len, n_ctx_len), F32),
                        pltpu.VMEM((2, n_ctx_len, n_ctx_len), BF)],
        compiler_params=_cparams(("parallel",)),
        name="attn_ctx",
    )(*ctx_args)


def _layer_norm(z, g, b):
    mu = jnp.mean(z, axis=-1, keepdims=True)
    zc = z - mu
    var = jnp.mean(zc * zc, axis=-1, keepdims=True)
    return zc * lax.rsqrt(var + NORM_EPS) * g + b


def _oproj_kernel(*refs, nctx, g_lo, n_x, sub):
    x_refs, refs = refs[:n_x], refs[n_x:]
    (ot_ref, wo_ref, g1_ref, lg_ref, lb_ref, sh2_ref, sc2_ref, wr_ref,
     x1_ref, h2_ref, aff_ref) = refs
    is_ctx = pl.program_id(0) + g_lo < nctx
    tm = x1_ref.shape[1]

    def project(i):
        return lax.dot_general(ot_ref[0, :, i * sub:(i + 1) * sub], wo_ref[...], _TN,
                               preferred_element_type=F32)

    def finish(i, y):
        rows = slice(i * sub, (i + 1) * sub)
        x = _act_rows(x_refs, is_ctx, rows)
        x1 = _layer_norm(DEEPNORM_ALPHA * x + g1_ref[0] * y, lg_ref[...], lb_ref[...])
        x1_ref[0, rows, :] = x1
        h2 = x1 * (1.0 + sc2_ref[0]) + sh2_ref[0]
        hb = h2.astype(BF)
        h2_ref[0, rows, :] = hb
        lg = lax.dot_general(wr_ref[...], hb, _NT, preferred_element_type=F32)
        e = jnp.exp(lg - jnp.max(lg, axis=0, keepdims=True))
        aff_ref[0, :, rows] = e / jnp.sum(e, axis=0, keepdims=True)

    y_prev = None
    for i in range(tm // sub + 1):
        y = project(i) if i < tm // sub else None
        if i >= 1:
            finish(i - 1, y_prev)
        y_prev = y


def _oproj_call(ot, x, mod, layer, nctx, rows_per_layer, w_o, ln_g, ln_b, w_router, g_lo,
                tm=OPROJ_ROWS, sub=OPROJ_SUB):
    x_specs, x_args = _act_specs(x, nctx, tm, g_lo)
    g_n = ot.shape[0]
    _, n, d = x_args[0].shape
    n_e = w_router.shape[1]
    wr_t = w_router.T.astype(BF)
    full = lambda shape: pl.BlockSpec(shape, lambda g, r: (0,) * len(shape))

    def mspec(chunk):
        def imap(g, r):
            gg = g + g_lo
            return (layer * rows_per_layer + jnp.where(gg < nctx, 0, gg - nctx + 1), 0, chunk)
        return pl.BlockSpec((1, 1, d), imap)

    return pl.pallas_call(
        functools.partial(_oproj_kernel, nctx=nctx, g_lo=g_lo, n_x=len(x_args), sub=sub),
        grid=(g_n - g_lo, n // tm),
        in_specs=x_specs + [
            pl.BlockSpec((1, ot.shape[1], tm), lambda g, r: (g + g_lo, 0, r)),
            full(w_o.shape),
            mspec(2),
            full((1, d)), full((1, d)),
            mspec(3), mspec(4),
            full((n_e, d)),
        ],
        out_specs=[
            pl.BlockSpec((1, tm, d), lambda g, r: (g, r, 0)),
            pl.BlockSpec((1, tm, d), lambda g, r: (g, r, 0)),
            pl.BlockSpec((1, n_e, tm), lambda g, r: (g, 0, r)),
        ],
        out_shape=[
            jax.ShapeDtypeStruct((g_n - g_lo, n, d), F32),
            jax.ShapeDtypeStruct((g_n - g_lo, n, d), BF),
            jax.ShapeDtypeStruct((g_n - g_lo, n_e, n), F32),
        ],
        compiler_params=_cparams(("parallel", "parallel")),
        name="oproj_norm_router",
    )(*x_args, ot, w_o.astype(BF), mod, ln_g.reshape(1, d), ln_b.reshape(1, d), mod, mod, wr_t)


def _route_segments(aff, seg_len, cap, tri):
    n_e, n = aff.shape
    nseg = n // seg_len
    bits = pltpu.bitcast(aff, jnp.int32)
    lane = lax.broadcasted_iota(jnp.int32, (n_e, n), 1)
    assert seg_len & (seg_len - 1) == 0
    pos = lane & (seg_len - 1)

    def seg_count(mask):
        v = jnp.where(mask, 1.0, 0.0)
        parts = []
        for s in range(nseg):
            c = jnp.sum(v[:, s * seg_len:(s + 1) * seg_len], axis=1, keepdims=True)
            parts.append(jnp.broadcast_to(c, (n_e, seg_len)))
        return parts[0] if nseg == 1 else jnp.concatenate(parts, axis=1)

    def thr_step(i, thr):
        cand = thr | (jnp.int32(1) << (30 - i))
        return jnp.where(seg_count(bits >= cand) >= cap, cand, thr)

    thr = lax.fori_loop(0, 31, thr_step, jnp.zeros((n_e, n), jnp.int32))
    gt = bits > thr
    eq = bits == thr
    need = cap - seg_count(gt)

    def tie_step(i, j):
        cand = j | (jnp.int32(1) << (seg_len.bit_length() - 2 - i))
        return jnp.where(seg_count(eq & (pos < cand)) < need, cand, j)

    j = lax.fori_loop(0, seg_len.bit_length() - 1, tie_step, jnp.zeros((n_e, n), jnp.int32))
    sel = gt | (eq & (pos <= j))

    selb = jnp.where(sel, 1.0, 0.0).astype(BF)
    tiles_per_seg = seg_len // LANE
    tiles_per_chunk = MOE_TOKEN_CHUNK // LANE
    lane1 = lax.broadcasted_iota(jnp.int32, (n_e, LANE), 1)
    starts = jnp.zeros((n_e, LANE), F32)
    out = []
    run = jnp.zeros((n_e, 1), F32)
    for t in range(n // LANE):
        if t % tiles_per_seg == 0:
            run = jnp.full((n_e, 1), float((t // tiles_per_seg) * cap), F32)
        if t % tiles_per_chunk == 0:
            starts = jnp.where(lane1 == t // tiles_per_chunk, run, starts)
        inc = jnp.dot(selb[:, t * LANE:(t + 1) * LANE], tri, preferred_element_type=F32)
        out.append(inc + (run - 1.0))
        run = run + inc[:, LANE - 1:LANE]
    starts = jnp.where(lane1 == n // MOE_TOKEN_CHUNK, run, starts)
    slot = jnp.concatenate(out, axis=1).astype(jnp.int32)
    return jnp.where(sel, slot, -1), starts.astype(jnp.int32)


def _route_kernel(aff_ref, tri_ref, slot_ref, starts_ref, *, nctx, n_ctx_len, g_lo):
    n = aff_ref.shape[2]
    gb, n_e = aff_ref.shape[0], aff_ref.shape[1]
    g = pl.program_id(0) * gb + g_lo

    def run(seg_len):
        slot, starts = _route_segments(aff_ref[...].reshape(gb * n_e, n), seg_len,
                                       CAPACITY_FACTOR * seg_len // N_EXPERTS, tri_ref[...])
        slot_ref[...] = slot.reshape(gb, n_e, n)
        starts_ref[...] = starts.reshape(gb, n_e, LANE)

    if g_lo < nctx:
        pl.when(g < nctx)(functools.partial(run, n_ctx_len))
    pl.when(g >= nctx)(functools.partial(run, n))


def _route_call(aff, nctx, n_ctx_len, g_lo):
    g_n, n_e, n = aff.shape
    assert n_ctx_len % MOE_TOKEN_CHUNK == 0 and n // MOE_TOKEN_CHUNK < LANE
    gb = ROUTE_GROUPS_PER_STEP
    if (nctx - g_lo) % gb or g_n % gb:
        gb = 1
    tri = jnp.triu(jnp.ones((LANE, LANE), F32)).astype(BF)
    return pl.pallas_call(
        functools.partial(_route_kernel, nctx=nctx, n_ctx_len=n_ctx_len, g_lo=g_lo),
        grid=(g_n // gb,),
        in_specs=[pl.BlockSpec((gb, n_e, n), lambda g: (g, 0, 0)),
                  pl.BlockSpec((LANE, LANE), lambda g: (0, 0))],
        out_specs=[pl.BlockSpec((gb, n_e, n), lambda g: (g, 0, 0)),
                   pl.BlockSpec((gb, n_e, LANE), lambda g: (g, 0, 0))],
        out_shape=[jax.ShapeDtypeStruct((g_n, n_e, n), jnp.int32),
                   jax.ShapeDtypeStruct((g_n, n_e, LANE), jnp.int32)],
        compiler_params=_cparams(("parallel",)),
        name="ec_route",
    )(aff, tri)


def _gather_pairs(starts, n_slots, n_chunks):
    n_blocks = n_slots // MOE_SLOT_BLOCK
    n_pairs = n_blocks + n_chunks - 1
    assert n_blocks <= 16 and n_chunks <= 16
    s = starts[..., :n_chunks + 1]
    lo = jnp.arange(n_blocks, dtype=jnp.int32) * MOE_SLOT_BLOCK
    c_lo = jnp.sum(s[..., None, :] <= lo[:, None], axis=-1) - 1
    c_hi = jnp.sum(s[..., None, :] <= lo[:, None] + (MOE_SLOT_BLOCK - 1), axis=-1) - 1
    cnt = c_hi - c_lo + 1
    first = jnp.cumsum(cnt, axis=-1) - cnt
    j = jnp.arange(n_pairs, dtype=jnp.int32)
    r = jnp.sum(first[..., None, :] <= j[:, None], axis=-1) - 1
    pick = r[..., None] == jnp.arange(n_blocks, dtype=jnp.int32)
    c = j + jnp.sum(jnp.where(pick, (c_lo - first)[..., None, :], 0), axis=-1)
    valid = j < jnp.sum(cnt, axis=-1, keepdims=True)
    return jnp.where(valid, 256 + r * 16 + c, 0).astype(jnp.int32).reshape(-1), n_pairs


def _moe_kernel(pairs_ref, h_ref, slot_ref, gate_ref, wg_ref, wu_ref, wd_ref,
                x1_ref, g2_ref, lg_ref, lb_ref, x2_ref, xs_scr, *, n_slots, n_pairs):
    g, e = pl.program_id(0), pl.program_id(1)
    n_e = pl.num_programs(1)
    n_chunks, chunk = slot_ref.shape[1], slot_ref.shape[2]
    sb = MOE_SLOT_BLOCK

    @pl.when(e == 0)
    def _():
        x2_ref[...] = jnp.zeros_like(x2_ref)

    xs_scr[...] = jnp.zeros_like(xs_scr)
    base = (g * n_e + e) * n_pairs
    row_iota = lax.broadcasted_iota(jnp.int32, (sb, chunk), 0)
    for j in range(n_pairs):
        code = pairs_ref[base + j]
        c = code & 15
        r = (code >> 4) & 15
        rel = jnp.where(code >= 256, slot_ref[0, pl.ds(c, 1), :] - r * sb, -1)
        onehot = jnp.where(row_iota == rel, 1.0, 0.0).astype(BF)
        tok0 = pl.multiple_of(c * chunk, chunk)
        s0 = pl.multiple_of(r * sb, sb)
        xs_scr[pl.ds(s0, sb), :] += jnp.dot(onehot, h_ref[0, pl.ds(tok0, chunk), :],
                                            preferred_element_type=F32)
    xs = xs_scr[...].astype(BF)

    a = jnp.dot(xs, wg_ref[0, 0], preferred_element_type=F32)
    u = jnp.dot(xs, wu_ref[0, 0], preferred_element_type=F32)
    hh = (a * jax.nn.sigmoid(a) * u).astype(BF)
    y = jnp.dot(hh, wd_ref[0, 0], preferred_element_type=F32).astype(BF)

    slot_iota = lax.broadcasted_iota(jnp.int32, (n_slots, chunk), 0)
    for c in range(n_chunks):
        rows = slice(c * chunk, (c + 1) * chunk)
        selg = jnp.where(slot_iota == slot_ref[0, c:c + 1, :], gate_ref[0, c:c + 1, :], 0.0).astype(BF)
        x2_ref[0, rows, :] += lax.dot_general(selg, y, _TN, preferred_element_type=F32)

    @pl.when(e == n_e - 1)
    def _():
        for c in range(n_chunks):
            rows = slice(c * chunk, (c + 1) * chunk)
            x2_ref[0, rows, :] = _layer_norm(DEEPNORM_ALPHA * x1_ref[0, rows, :] + g2_ref[0] * x2_ref[0, rows, :],
                                             lg_ref[...], lb_ref[...])


def _moe_call(h2, slot, starts, aff, w_gate, w_up, w_down, n_slots,
              x1, mod, layer, nctx, rows_per_layer, ln_g, ln_b, g_lo):
    g_n, n, d = h2.shape
    _, n_e, _, ff = w_gate.shape
    n_chunks = n // MOE_TOKEN_CHUNK
    slot3 = slot.reshape(g_n * n_e, n_chunks, MOE_TOKEN_CHUNK)
    gate3 = aff.reshape(g_n * n_e, n_chunks, MOE_TOKEN_CHUNK)
    pairs, n_pairs = _gather_pairs(starts, n_slots, n_chunks)

    def mod_map(g, e, p):
        gg = g + g_lo
        return (layer * rows_per_layer + jnp.where(gg < nctx, 0, gg - nctx + 1), 0, 5)

    full = lambda shape: pl.BlockSpec(shape, lambda g, e, p: (0,) * len(shape))
    grid_spec = pltpu.PrefetchScalarGridSpec(
        num_scalar_prefetch=1,
        grid=(g_n, n_e),
        in_specs=[
            pl.BlockSpec((1, n, d), lambda g, e, p: (g, 0, 0)),
            pl.BlockSpec((1, n_chunks, MOE_TOKEN_CHUNK), lambda g, e, p: (g * n_e + e, 0, 0)),
            pl.BlockSpec((1, n_chunks, MOE_TOKEN_CHUNK), lambda g, e, p: (g * n_e + e, 0, 0)),
            pl.BlockSpec((1, 1, d, ff), lambda g, e, p: (layer, e, 0, 0)),
            pl.BlockSpec((1, 1, d, ff), lambda g, e, p: (layer, e, 0, 0)),
            pl.BlockSpec((1, 1, ff, d), lambda g, e, p: (layer, e, 0, 0)),
            pl.BlockSpec((1, n, d), lambda g, e, p: (g, 0, 0)),
            pl.BlockSpec((1, 1, d), mod_map),
            full((1, d)), full((1, d)),
        ],
        out_specs=pl.BlockSpec((1, n, d), lambda g, e, p: (g, 0, 0)),
        scratch_shapes=[pltpu.VMEM((n_slots, d), F32)],
    )
    return pl.pallas_call(
        functools.partial(_moe_kernel, n_slots=n_slots, n_pairs=n_pairs),
        grid_spec=grid_spec,
        out_shape=jax.ShapeDtypeStruct((g_n, n, d), F32),
        compiler_params=_cparams(("parallel", "arbitrary")),
        name="ec_moe",
    )(pairs, h2, slot3, gate3, w_gate, w_up, w_down, x1, mod, ln_g.reshape(1, d), ln_b.reshape(1, d))


def kernel(x, c, ctx, c_ctx, ada_w, ada_b, ln_mix_g, ln_mix_b, ln_ffn_g, ln_ffn_b, router_w, expert_w_gate, expert_w_up, expert_w_down, gqa_w_qkv, gqa_q_g, gqa_k_g, gqa_w_o, mla_w_dq, mla_q_g, mla_w_uq, mla_w_dkv, mla_kv_g, mla_w_ukv, mla_w_o):
    batch, n, d = x.shape
    n_ctx_len = ctx.shape[1]
    depth = ada_w.shape[0]
    assert n % n_ctx_len == 0 and (batch * n_ctx_len) % n == 0
    nctx = batch * n_ctx_len // n
    n_slots = CAPACITY_FACTOR * n // N_EXPERTS

    rows = -(-(batch + 1) // MOD_ROWS_PAD) * MOD_ROWS_PAD
    cvec = jnp.concatenate([c_ctx[None, :], c, jnp.zeros((rows - batch - 1, d), F32)], axis=0)
    mod = _ada_call(cvec, ada_w, ada_b).reshape(depth * rows, 1, 6 * d)

    w_gate, w_up, w_down = (w.astype(BF) for w in (expert_w_gate, expert_w_up, expert_w_down))
    xs = (ctx.reshape(nctx, n, d), x)
    for i in range(depth):
        last = i == depth - 1
        j = i // N_MIXERS
        if i % N_MIXERS == 0:
            qt, k, vt = _gqa_proj_call(xs, mod, i, nctx, rows, gqa_w_qkv[j], gqa_q_g[j], gqa_k_g[j])
            ot = _attn_call(qt, k, None, vt, nctx, n_ctx_len, GQA_HEADS, GQA_KV_HEADS, not last)
            w_o = gqa_w_o[j]
        else:
            qt, kn, kr, vt = _mla_proj_call(xs, mod, i, nctx, rows, mla_w_dq[j], mla_q_g[j], mla_w_uq[j],
                                            mla_w_dkv[j], mla_kv_g[j], mla_w_ukv[j])
            ot = _attn_call(qt, kn, kr, vt, nctx, n_ctx_len, MLA_HEADS, MLA_HEADS, not last)
            w_o = mla_w_o[j]
        g_lo = nctx if last else 0
        x1, h2, aff = _oproj_call(ot, xs, mod, i, nctx, rows, w_o, ln_mix_g[i], ln_mix_b[i],
                                  router_w[i], g_lo)
        slot, starts = _route_call(aff, nctx, n_ctx_len, g_lo)
        xs = _moe_call(h2, slot, starts, aff, w_gate, w_up, w_down, n_slots,
                       x1, mod, i, nctx, rows, ln_ffn_g[i], ln_ffn_b[i], g_lo)
    return xs
```

```python
import functools

import jax
import jax.numpy as jnp
import numpy as np
from jax import lax
from jax.experimental import pallas as pl
from jax.experimental.pallas import tpu as pltpu

D_MODEL = 1024
DEPTH = 4
GRID_W = 64
ROPE_THETA = 10000.0
NORM_EPS = 1e-6
GQA_HEADS = 8
GQA_KV_HEADS = 2
GQA_HEAD_DIM = 128
MLA_HEADS = 8
MLA_Q_RANK = 768
MLA_KV_RANK = 256
MLA_NOPE_DIM = 128
MLA_ROPE_DIM = 64
MLA_V_DIM = 128
N_EXPERTS = 16
EXPERT_FF = 1024
CAPACITY_FACTOR = 2
N_MIXERS = 2
DEEPNORM_ALPHA = (2 * DEPTH) ** 0.25
LOG2E = 1.4426950408889634

LANE = 128
SUBLANE = 8
MOD_ROWS_PAD = 8
VMEM_LIMIT = 58 * 1024 * 1024

ADA_COLS = 1536
GQA_PROJ_ROWS, GQA_PROJ_SUB = 2048, 256
MLA_PROJ_ROWS, MLA_PROJ_SUB = 1024, 256
OPROJ_ROWS, OPROJ_SUB = 1024, 256
ATTN_Q_STEP, ATTN_Q_UNIT = 1024, 512
ATTN_KEY_CHUNK = 1024
ATTN_EXP_ROWS = 128
ROUTE_GROUPS_PER_STEP = 4
MOE_TOKEN_CHUNK = 256
MOE_SLOT_BLOCK = 128

BF = jnp.bfloat16
F32 = jnp.float32

_NT = (((1,), (1,)), ((), ()))
_TN = (((0,), (0,)), ((), ()))


def _cparams(sem):
    return pltpu.CompilerParams(dimension_semantics=sem, vmem_limit_bytes=VMEM_LIMIT)


def _ada_kernel(c_ref, w_ref, b_ref, o_ref):
    c = c_ref[...]
    s = (c * jax.nn.sigmoid(c)).astype(BF)
    o_ref[0] = jnp.dot(s, w_ref[0].astype(BF), preferred_element_type=F32) + b_ref[0]


def _ada_call(cvec, ada_w, ada_b):
    n_layers, d, n_out = ada_w.shape
    rows = cvec.shape[0]
    tn = ADA_COLS
    return pl.pallas_call(
        _ada_kernel,
        grid=(n_layers, n_out // tn),
        in_specs=[
            pl.BlockSpec((rows, d), lambda i, j: (0, 0)),
            pl.BlockSpec((1, d, tn), lambda i, j: (i, 0, j)),
            pl.BlockSpec((1, 1, tn), lambda i, j: (i, 0, j)),
        ],
        out_specs=pl.BlockSpec((1, rows, tn), lambda i, j: (i, 0, j)),
        out_shape=jax.ShapeDtypeStruct((n_layers, rows, n_out), F32),
        compiler_params=_cparams(("parallel", "parallel")),
        name="ada_mod",
    )(cvec, ada_w, ada_b.reshape(n_layers, 1, n_out))


def _rope_tables(n, rot_dim):
    t = jnp.arange(n, dtype=jnp.int32)
    row = (t // GRID_W).astype(F32)
    col = (t % GRID_W).astype(F32)
    axis_dim = rot_dim // 2
    freqs = ROPE_THETA ** (-jnp.arange(0, axis_dim, 2, dtype=F32) / axis_dim)
    ang = jnp.concatenate([row[:, None] * freqs, col[:, None] * freqs], axis=-1)
    return jnp.cos(ang), jnp.sin(ang)


def _act_specs(x, nctx, tm, g_lo=0):
    if not isinstance(x, tuple):
        return [pl.BlockSpec((1, tm, x.shape[2]), lambda g, r: (g + g_lo, r, 0))], [x]
    xc, xl = x
    if g_lo == nctx:
        return [pl.BlockSpec((1, tm, xl.shape[2]), lambda g, r: (g, r, 0))], [xl]
    assert g_lo == 0
    last_r = xc.shape[1] // tm - 1
    spec_c = pl.BlockSpec((1, tm, xc.shape[2]),
                          lambda g, r: (jnp.minimum(g, nctx - 1), jnp.where(g < nctx, r, last_r), 0))
    spec_l = pl.BlockSpec((1, tm, xl.shape[2]),
                          lambda g, r: (jnp.maximum(g - nctx, 0), jnp.where(g < nctx, 0, r), 0))
    return [spec_c, spec_l], [xc, xl]


def _act_rows(x_refs, is_ctx, rows):
    if len(x_refs) == 1:
        return x_refs[0][0, rows, :]
    return jnp.where(is_ctx, x_refs[0][0, rows, :], x_refs[1][0, rows, :])


def _gqa_proj_kernel(*refs, nctx, n_x, sub):
    x_refs, refs = refs[:n_x], refs[n_x:]
    (sh_ref, sc_ref, wqv_ref, wk_ref, gq_ref, gk_ref, cs_ref, sn_ref, cst_ref, snt_ref,
     qt_ref, k_ref, vt_ref) = refs
    is_ctx = pl.program_id(0) < nctx
    hd, half = GQA_HEAD_DIM, GQA_HEAD_DIM // 2
    n_sub = k_ref.shape[1] // sub

    def project(i):
        rows = slice(i * sub, (i + 1) * sub)
        h = (_act_rows(x_refs, is_ctx, rows) * (1.0 + sc_ref[0]) + sh_ref[0]).astype(BF)
        qv = lax.dot_general(wqv_ref[...], h, _NT, preferred_element_type=F32)
        kk = jnp.dot(h, wk_ref[...], preferred_element_type=F32)
        return qv, kk

    def finish(i, qv, kk):
        rows = slice(i * sub, (i + 1) * sub)
        cost = jnp.where(is_ctx, 1.0, cst_ref[:, rows])
        sint = jnp.where(is_ctx, 0.0, snt_ref[:, rows])
        for u in range(GQA_HEADS):
            q = qv[u * hd:(u + 1) * hd]
            ms = jnp.mean(q * q, axis=0, keepdims=True)
            qn = q * lax.rsqrt(ms + NORM_EPS) * gq_ref[...]
            x0, x1 = qn[:half], qn[half:]
            qt_ref[0, u * hd:u * hd + half, rows] = (x0 * cost - x1 * sint).astype(BF)
            qt_ref[0, u * hd + half:(u + 1) * hd, rows] = (x0 * sint + x1 * cost).astype(BF)
        vt_ref[0, :, rows] = qv[GQA_HEADS * hd:].astype(BF)
        cos = jnp.where(is_ctx, 1.0, cs_ref[rows, :])
        sin = jnp.where(is_ctx, 0.0, sn_ref[rows, :])
        for j in range(GQA_KV_HEADS):
            k = kk[:, j * hd:(j + 1) * hd]
            ms = jnp.mean(k * k, axis=1, keepdims=True)
            kn = k * lax.rsqrt(ms + NORM_EPS) * gk_ref[...]
            k_ref[0, rows, j * hd:(j + 1) * hd] = (kn * cos + pltpu.roll(kn, half, 1) * sin).astype(BF)

    prev = None
    for i in range(n_sub + 1):
        cur = project(i) if i < n_sub else None
        if i >= 1:
            finish(i - 1, *prev)
        prev = cur


def _mod_spec(layer, chunk, nctx, rows_per_layer, d):
    def imap(g, *_):
        return (layer * rows_per_layer + jnp.where(g < nctx, 0, g - nctx + 1), 0, chunk)
    return pl.BlockSpec((1, 1, d), imap)


def _gqa_proj_call(x, mod, layer, nctx, rows_per_layer, w_qkv, q_g, k_g,
                   tm=GQA_PROJ_ROWS, sub=GQA_PROJ_SUB):
    x_specs, x_args = _act_specs(x, nctx, tm)
    g_n = sum(a.shape[0] for a in x_args)
    _, n, d = x_args[0].shape
    hd, half = GQA_HEAD_DIM, GQA_HEAD_DIM // 2
    nq, nk = GQA_HEADS * hd, GQA_KV_HEADS * hd
    perm = np.concatenate([np.arange(0, hd, 2), np.arange(1, hd, 2)])
    wq = w_qkv[:, :nq].reshape(d, GQA_HEADS, hd)[:, :, perm].reshape(d, nq)
    wk = w_qkv[:, nq:nq + nk].reshape(d, GQA_KV_HEADS, hd)[:, :, perm].reshape(d, nk)
    wv = w_qkv[:, nq + nk:]
    wqv_t = jnp.concatenate([wq, wv], axis=1).T.astype(BF)
    wk = wk.astype(BF)
    gq = (q_g[perm] * (hd ** -0.5 * LOG2E)).reshape(hd, 1)
    gk = k_g[perm].reshape(1, hd)
    cos, sin = _rope_tables(n, hd)
    cs = jnp.concatenate([cos, cos], axis=1)
    sn = jnp.concatenate([-sin, sin], axis=1)
    kernel = functools.partial(_gqa_proj_kernel, nctx=nctx, n_x=len(x_args), sub=sub)
    full = lambda shape: pl.BlockSpec(shape, lambda g, r: (0,) * len(shape))
    return pl.pallas_call(
        kernel,
        grid=(g_n, n // tm),
        in_specs=x_specs + [
            _mod_spec(layer, 0, nctx, rows_per_layer, d),
            _mod_spec(layer, 1, nctx, rows_per_layer, d),
            full((nq + nk, d)),
            full((d, nk)),
            full((hd, 1)),
            full((1, hd)),
            pl.BlockSpec((tm, hd), lambda g, r: (r, 0)),
            pl.BlockSpec((tm, hd), lambda g, r: (r, 0)),
            pl.BlockSpec((half, tm), lambda g, r: (0, r)),
            pl.BlockSpec((half, tm), lambda g, r: (0, r)),
        ],
        out_specs=[
            pl.BlockSpec((1, nq, tm), lambda g, r: (g, 0, r)),
            pl.BlockSpec((1, tm, nk), lambda g, r: (g, r, 0)),
            pl.BlockSpec((1, nk, tm), lambda g, r: (g, 0, r)),
        ],
        out_shape=[
            jax.ShapeDtypeStruct((g_n, nq, n), BF),
            jax.ShapeDtypeStruct((g_n, n, nk), BF),
            jax.ShapeDtypeStruct((g_n, nk, n), BF),
        ],
        compiler_params=_cparams(("parallel", "parallel")),
        name="gqa_proj",
    )(*x_args, mod, mod, wqv_t, wk, gq, gk, cs, sn, cos.T, sin.T)


MLA_QK_PAD = 256


def _mla_proj_kernel(x_ref, sh_ref, sc_ref, wdq_ref, gq_ref, wuq_ref, wdkv_ref, gkv_ref,
                     wukn_ref, wuv_ref, cs_ref, sn_ref, cst_ref, snt_ref,
                     qt_ref, kn_ref, kr_ref, vt_ref, *, nctx, sub):
    is_ctx = pl.program_id(0) < nctx
    hr = MLA_ROPE_DIM // 2
    n_sub = kn_ref.shape[1] // sub

    def down(i):
        rows = slice(i * sub, (i + 1) * sub)
        h = (x_ref[0, rows, :] * (1.0 + sc_ref[0]) + sh_ref[0]).astype(BF)
        cq = jnp.dot(h, wdq_ref[...], preferred_element_type=F32)
        ckv = jnp.dot(h, wdkv_ref[...], preferred_element_type=F32)
        return cq, ckv

    def up(i, cq, ckv):
        rows = slice(i * sub, (i + 1) * sub)
        ms = jnp.mean(cq * cq, axis=1, keepdims=True)
        cqn = (cq * lax.rsqrt(ms + NORM_EPS) * gq_ref[...]).astype(BF)
        qt = lax.dot_general(wuq_ref[...], cqn, _NT, preferred_element_type=F32)
        c = ckv[:, :MLA_KV_RANK]
        ms = jnp.mean(c * c, axis=1, keepdims=True)
        cn = (c * lax.rsqrt(ms + NORM_EPS) * gkv_ref[...]).astype(BF)
        kr = ckv[:, MLA_KV_RANK:]
        cos = jnp.where(is_ctx, 1.0, cs_ref[rows, :])
        sin = jnp.where(is_ctx, 0.0, sn_ref[rows, :])
        kr_ref[0, rows, :] = (kr * cos + pltpu.roll(kr, LANE // 2, 1) * sin).astype(BF)
        kn_ref[0, rows, :] = jnp.dot(cn, wukn_ref[...], preferred_element_type=F32).astype(BF)
        vt_ref[0, :, rows] = lax.dot_general(wuv_ref[...], cn, _NT, preferred_element_type=F32).astype(BF)
        return qt

    def rope(i, qt):
        rows = slice(i * sub, (i + 1) * sub)
        qt = qt * ((MLA_NOPE_DIM + MLA_ROPE_DIM) ** -0.5 * LOG2E)
        cost = jnp.where(is_ctx, 1.0, cst_ref[:, rows])
        sint = jnp.where(is_ctx, 0.0, snt_ref[:, rows])
        for u in range(MLA_HEADS):
            src = u * (MLA_NOPE_DIM + MLA_ROPE_DIM)
            base = u * MLA_QK_PAD
            qt_ref[0, base:base + MLA_NOPE_DIM, rows] = qt[src:src + MLA_NOPE_DIM].astype(BF)
            r0 = base + MLA_NOPE_DIM
            x0 = qt[src + MLA_NOPE_DIM:src + MLA_NOPE_DIM + hr]
            x1 = qt[src + MLA_NOPE_DIM + hr:src + MLA_NOPE_DIM + 2 * hr]
            zero = jnp.zeros_like(x0).astype(BF)
            qt_ref[0, r0:r0 + hr, rows] = (x0 * cost - x1 * sint).astype(BF)
            qt_ref[0, r0 + hr:r0 + 2 * hr, rows] = zero
            qt_ref[0, r0 + 2 * hr:r0 + 3 * hr, rows] = (x0 * sint + x1 * cost).astype(BF)
            qt_ref[0, r0 + 3 * hr:r0 + 4 * hr, rows] = zero

    d_prev, q_prev = None, None
    for i in range(n_sub + 2):
        d_cur = down(i) if i < n_sub else None
        q_cur = up(i - 1, *d_prev) if 1 <= i <= n_sub else None
        if i >= 2:
            rope(i - 2, q_prev)
        d_prev, q_prev = d_cur, q_cur


def _mla_proj_call(x, mod, layer, nctx, rows_per_layer, w_dq, q_g, w_uq, w_dkv, kv_g, w_ukv,
                   tm=MLA_PROJ_ROWS, sub=MLA_PROJ_SUB):
    g_n, n, d = x.shape
    nh, dn, dr, dv = MLA_HEADS, MLA_NOPE_DIM, MLA_ROPE_DIM, MLA_V_DIM
    hr = dr // 2
    ev, od = np.arange(0, dr, 2), np.arange(1, dr, 2)
    wu = w_uq.reshape(MLA_Q_RANK, nh, dn + dr)
    wu = jnp.concatenate([wu[:, :, :dn], wu[:, :, dn + ev], wu[:, :, dn + od]], axis=2)
    wuq_t = wu.reshape(MLA_Q_RANK, nh * (dn + dr)).T.astype(BF)
    zk = jnp.zeros((d, hr), F32)
    wdkv = jnp.concatenate([w_dkv[:, :MLA_KV_RANK], w_dkv[:, MLA_KV_RANK + ev], zk,
                            w_dkv[:, MLA_KV_RANK + od], zk], axis=1).astype(BF)
    wkv = w_ukv.reshape(MLA_KV_RANK, nh, dn + dv)
    wukn = wkv[:, :, :dn].reshape(MLA_KV_RANK, nh * dn).astype(BF)
    wuv_t = wkv[:, :, dn:].reshape(MLA_KV_RANK, nh * dv).T.astype(BF)
    cos, sin = _rope_tables(n, dr)
    z = jnp.zeros_like(cos)
    cs = jnp.concatenate([cos, z, cos, z], axis=1)
    sn = jnp.concatenate([-sin, z, sin, z], axis=1)
    kernel = functools.partial(_mla_proj_kernel, nctx=nctx, sub=sub)
    full = lambda shape: pl.BlockSpec(shape, lambda g, r: (0,) * len(shape))
    return pl.pallas_call(
        kernel,
        grid=(g_n, n // tm),
        in_specs=[
            pl.BlockSpec((1, tm, d), lambda g, r: (g, r, 0)),
            _mod_spec(layer, 0, nctx, rows_per_layer, d),
            _mod_spec(layer, 1, nctx, rows_per_layer, d),
            full((d, MLA_Q_RANK)),
            full((1, MLA_Q_RANK)),
            full((nh * (dn + dr), MLA_Q_RANK)),
            full((d, MLA_KV_RANK + LANE)),
            full((1, MLA_KV_RANK)),
            full((MLA_KV_RANK, nh * dn)),
            full((nh * dv, MLA_KV_RANK)),
            pl.BlockSpec((tm, LANE), lambda g, r: (r, 0)),
            pl.BlockSpec((tm, LANE), lambda g, r: (r, 0)),
            pl.BlockSpec((hr, tm), lambda g, r: (0, r)),
            pl.BlockSpec((hr, tm), lambda g, r: (0, r)),
        ],
        out_specs=[
            pl.BlockSpec((1, nh * MLA_QK_PAD, tm), lambda g, r: (g, 0, r)),
            pl.BlockSpec((1, tm, nh * dn), lambda g, r: (g, r, 0)),
            pl.BlockSpec((1, tm, LANE), lambda g, r: (g, r, 0)),
            pl.BlockSpec((1, nh * dv, tm), lambda g, r: (g, 0, r)),
        ],
        out_shape=[
            jax.ShapeDtypeStruct((g_n, nh * MLA_QK_PAD, n), BF),
            jax.ShapeDtypeStruct((g_n, n, nh * dn), BF),
            jax.ShapeDtypeStruct((g_n, n, LANE), BF),
            jax.ShapeDtypeStruct((g_n, nh * dv, n), BF),
        ],
        compiler_params=_cparams(("parallel", "parallel")),
        name="mla_proj",
    )(x, mod, mod, w_dq.astype(BF), q_g.reshape(1, -1), wuq_t, wdkv, kv_g.reshape(1, -1),
      wukn, wuv_t, cs, sn, cos.T, sin.T)


def _attn_kernel(*refs, mla, with_lat, hb, rep, dq, dv, ck, sb, n_cast=0):
    refs = list(refs)
    qt_ref = refs.pop(0)
    kl_ref = refs.pop(0) if with_lat else None
    kc_ref = refs.pop(0)
    krl_ref = refs.pop(0) if (mla and with_lat) else None
    krc_ref = refs.pop(0) if mla else None
    vtl_ref = refs.pop(0) if with_lat else None
    vtc_ref = refs.pop(0)
    p_scr = refs.pop()
    s_scr = refs.pop()
    cast_out = [refs.pop() for _ in range(n_cast)][::-1]
    ot_ref = refs.pop()
    cast_in = refs[len(refs) - n_cast:]

    for w_in, w_out in zip(cast_in, cast_out):
        w_out[...] = w_in[0].astype(BF)

    tq = s_scr.shape[2]

    def stage_a(t, unit, ci):
        h, q0, chunks = unit
        k_ref, kr_ref, _, r0, rows, s0 = chunks[ci]
        j = h // rep
        keys = k_ref[0, r0:r0 + rows, j * LANE:(j + 1) * LANE]
        if mla:
            keys = jnp.concatenate([keys, kr_ref[0, r0:r0 + rows, :]], axis=1)
        s = jnp.dot(keys, qt_ref[0, h * dq:(h + 1) * dq, q0:q0 + tq],
                    preferred_element_type=F32)
        s_scr[t % 2, s0:s0 + rows, :] = s
        return jnp.max(s.reshape(rows // SUBLANE, SUBLANE, tq), axis=0)

    def stage_b(t, unit, ci, m):
        h, _, chunks = unit
        _, _, vt_ref, r0, rows, s0 = chunks[ci]
        j = h // rep
        l8 = None
        for r in range(0, rows, sb):
            p = jnp.exp2(s_scr[t % 2, s0 + r:s0 + r + sb, :] - m)
            ps = jnp.sum(p.reshape(sb // SUBLANE, SUBLANE, tq), axis=0)
            l8 = ps if l8 is None else l8 + ps
            p_scr[ci % 2, r:r + sb, :] = p.astype(BF)
        pv = jnp.dot(vt_ref[0, j * dv:(j + 1) * dv, r0:r0 + rows], p_scr[ci % 2, :rows, :],
                     preferred_element_type=F32)
        return l8, pv

    if with_lat:
        n_ctx = kc_ref.shape[1]
        chunks = [(kc_ref, krc_ref, vtc_ref, 0, n_ctx, 0)]
        for r0 in range(0, kl_ref.shape[1], ck):
            chunks.append((kl_ref, krl_ref, vtl_ref, r0, ck, n_ctx + r0))
        units = [(h, q0, chunks) for q0 in range(0, qt_ref.shape[2], tq) for h in range(hb)]
    else:
        units = [(h, s * tq, [(kc_ref, krc_ref, vtc_ref, s * tq, tq, 0)])
                 for s in range(kc_ref.shape[1] // tq) for h in range(hb)]

    m_cur = None
    for t in range(len(units) + 1):
        m_parts, l_parts, acc = [], [], None
        for ci in range(len(units[0][2])):
            if t < len(units):
                m_parts.append(stage_a(t, units[t], ci))
            if t >= 1:
                l8, pv = stage_b(t - 1, units[t - 1], ci, m_cur)
                l_parts.append(l8)
                acc = pv if acc is None else acc + pv
        if t >= 1:
            h, q0, _ = units[t - 1]
            l = jnp.sum(functools.reduce(jnp.add, l_parts), axis=0, keepdims=True)
            ot_ref[0, h * dv:(h + 1) * dv, q0:q0 + tq] = (acc / l).astype(BF)
        if t < len(units):
            m_cur = jnp.max(functools.reduce(jnp.maximum, m_parts), axis=0, keepdims=True)


def _attn_call(qt, k, kr, vt, nctx, n_ctx_len, n_heads, n_kv_heads, with_ctx_out, layer, expert_weights,
               tq=ATTN_Q_UNIT, tqb=ATTN_Q_STEP, ck=ATTN_KEY_CHUNK, sb=ATTN_EXP_ROWS):
    g_n, _, n = qt.shape
    dq = qt.shape[1] // n_heads
    dv = vt.shape[1] // n_kv_heads
    batch = g_n - nctx
    spg = n // n_ctx_len
    rep = n_heads // n_kv_heads
    hb = n_heads
    hkv = hb // rep
    mla = kr is not None
    ot_shape = jax.ShapeDtypeStruct((g_n, n_heads * dv, n), BF)
    kern = functools.partial(_attn_kernel, mla=mla, hb=hb, rep=rep, dq=dq, dv=dv, ck=ck, sb=sb)

    lat_in = [pl.BlockSpec((1, hb * dq, tqb), lambda b, h, t: (nctx + b, h, t)),
              pl.BlockSpec((1, n, hkv * LANE), lambda b, h, t: (nctx + b, 0, h)),
              pl.BlockSpec((1, n_ctx_len, hkv * LANE), lambda b, h, t: (b // spg, b % spg, h))]
    lat_args = [qt, k, k]
    if mla:
        lat_in += [pl.BlockSpec((1, n, LANE), lambda b, h, t: (nctx + b, 0, 0)),
                   pl.BlockSpec((1, n_ctx_len, LANE), lambda b, h, t: (b // spg, b % spg, 0))]
        lat_args += [kr, kr]
    lat_in += [pl.BlockSpec((1, hkv * dv, n), lambda b, h, t: (nctx + b, h, 0)),
               pl.BlockSpec((1, hkv * dv, n_ctx_len), lambda b, h, t: (b // spg, h, b % spg))]
    lat_args += [vt, vt]
    n_steps = batch * (n // tqb)
    n_t = n // tqb
    cast_out_specs, cast_out_shapes = [], []
    for w in expert_weights:
        n_l, n_e, a, b_ = w.shape
        assert (n_e * a) % n_steps == 0 and (n_e * a // n_steps) % (2 * SUBLANE) == 0
        slab = n_e * a // n_steps
        lat_in.append(pl.BlockSpec((1, slab, b_), lambda b, h, t: (layer, b * n_t + t, 0)))
        lat_args.append(w.reshape(n_l, n_e * a, b_))
        cast_out_specs.append(pl.BlockSpec((slab, b_), lambda b, h, t: (b * n_t + t, 0)))
        cast_out_shapes.append(jax.ShapeDtypeStruct((n_e * a, b_), BF))
    ot, *w_bf = pl.pallas_call(
        functools.partial(kern, with_lat=True, n_cast=len(expert_weights)),
        grid=(batch, n_heads // hb, n_t),
        in_specs=lat_in,
        out_specs=[pl.BlockSpec((1, hb * dv, tqb), lambda b, h, t: (nctx + b, h, t))] + cast_out_specs,
        out_shape=[ot_shape] + cast_out_shapes,
        scratch_shapes=[pltpu.VMEM((2, n_ctx_len + n, tq), F32),
                        pltpu.VMEM((2, max(ck, n_ctx_len), tq), BF)],
        compiler_params=_cparams(("parallel", "parallel", "parallel")),
        name="attn_latent",
    )(*lat_args)
    w_bf = [wb.reshape(w.shape[1:]) for wb, w in zip(w_bf, expert_weights)]
    if not with_ctx_out:
        return ot, w_bf

    ctx_in = [pl.BlockSpec((1, hb * dq, n), lambda g: (g, 0, 0)),
              pl.BlockSpec((1, n, hkv * LANE), lambda g: (g, 0, 0))]
    ctx_args = [qt, k]
    if mla:
        ctx_in += [pl.BlockSpec((1, n, LANE), lambda g: (g, 0, 0))]
        ctx_args += [kr]
    ctx_in += [pl.BlockSpec((1, hkv * dv, n), lambda g: (g, 0, 0)),
               pl.BlockSpec(memory_space=pl.ANY)]
    ctx_args += [vt, ot]
    ot = pl.pallas_call(
        functools.partial(kern, with_lat=False),
        grid=(nctx,),
        in_specs=ctx_in,
        out_specs=pl.BlockSpec((1, hb * dv, n), lambda g: (g, 0, 0)),
        out_shape=ot_shape,
        input_output_aliases={len(ctx_args) - 1: 0},
        scratch_shapes=[pltpu.VMEM((2, n_ctx_len, n_ctx_len), F32),
                        pltpu.VMEM((2, n_ctx_len, n_ctx_len), BF)],
        compiler_params=_cparams(("parallel",)),
        name="attn_ctx",
    )(*ctx_args)
    return ot, w_bf


def _layer_norm(z, g, b):
    mu = jnp.mean(z, axis=-1, keepdims=True)
    zc = z - mu
    var = jnp.mean(zc * zc, axis=-1, keepdims=True)
    return zc * lax.rsqrt(var + NORM_EPS) * g + b


def _oproj_kernel(*refs, nctx, g_lo, n_x, sub):
    x_refs, refs = refs[:n_x], refs[n_x:]
    (ot_ref, wo_ref, g1_ref, lg_ref, lb_ref, sh2_ref, sc2_ref, wr_ref,
     x1_ref, h2_ref, aff_ref) = refs
    is_ctx = pl.program_id(0) + g_lo < nctx
    tm = x1_ref.shape[1]

    def project(i):
        return lax.dot_general(ot_ref[0, :, i * sub:(i + 1) * sub], wo_ref[...], _TN,
                               preferred_element_type=F32)

    def finish(i, y):
        rows = slice(i * sub, (i + 1) * sub)
        x = _act_rows(x_refs, is_ctx, rows)
        x1 = _layer_norm(DEEPNORM_ALPHA * x + g1_ref[0] * y, lg_ref[...], lb_ref[...])
        x1_ref[0, rows, :] = x1
        h2 = x1 * (1.0 + sc2_ref[0]) + sh2_ref[0]
        hb = h2.astype(BF)
        h2_ref[0, rows, :] = hb
        lg = lax.dot_general(wr_ref[...], hb, _NT, preferred_element_type=F32)
        e = jnp.exp(lg - jnp.max(lg, axis=0, keepdims=True))
        aff_ref[0, :, rows] = e / jnp.sum(e, axis=0, keepdims=True)

    y_prev = None
    for i in range(tm // sub + 1):
        y = project(i) if i < tm // sub else None
        if i >= 1:
            finish(i - 1, y_prev)
        y_prev = y


def _oproj_call(ot, x, mod, layer, nctx, rows_per_layer, w_o, ln_g, ln_b, w_router, g_lo,
                tm=OPROJ_ROWS, sub=OPROJ_SUB):
    x_specs, x_args = _act_specs(x, nctx, tm, g_lo)
    g_n = ot.shape[0]
    _, n, d = x_args[0].shape
    n_e = w_router.shape[1]
    wr_t = w_router.T.astype(BF)
    full = lambda shape: pl.BlockSpec(shape, lambda g, r: (0,) * len(shape))

    def mspec(chunk):
        def imap(g, r):
            gg = g + g_lo
            return (layer * rows_per_layer + jnp.where(gg < nctx, 0, gg - nctx + 1), 0, chunk)
        return pl.BlockSpec((1, 1, d), imap)

    return pl.pallas_call(
        functools.partial(_oproj_kernel, nctx=nctx, g_lo=g_lo, n_x=len(x_args), sub=sub),
        grid=(g_n - g_lo, n // tm),
        in_specs=x_specs + [
            pl.BlockSpec((1, ot.shape[1], tm), lambda g, r: (g + g_lo, 0, r)),
            full(w_o.shape),
            mspec(2),
            full((1, d)), full((1, d)),
            mspec(3), mspec(4),
            full((n_e, d)),
        ],
        out_specs=[
            pl.BlockSpec((1, tm, d), lambda g, r: (g, r, 0)),
            pl.BlockSpec((1, tm, d), lambda g, r: (g, r, 0)),
            pl.BlockSpec((1, n_e, tm), lambda g, r: (g, 0, r)),
        ],
        out_shape=[
            jax.ShapeDtypeStruct((g_n - g_lo, n, d), F32),
            jax.ShapeDtypeStruct((g_n - g_lo, n, d), BF),
            jax.ShapeDtypeStruct((g_n - g_lo, n_e, n), F32),
        ],
        compiler_params=_cparams(("parallel", "parallel")),
        name="oproj_norm_router",
    )(*x_args, ot, w_o.astype(BF), mod, ln_g.reshape(1, d), ln_b.reshape(1, d), mod, mod, wr_t)


def _route_segments(aff, seg_len, cap, tri):
    n_e, n = aff.shape
    nseg = n // seg_len
    bits = pltpu.bitcast(aff, jnp.int32)
    lane = lax.broadcasted_iota(jnp.int32, (n_e, n), 1)
    assert seg_len & (seg_len - 1) == 0
    pos = lane & (seg_len - 1)

    def seg_count(mask):
        v = jnp.where(mask, 1.0, 0.0)
        parts = []
        for s in range(nseg):
            c = jnp.sum(v[:, s * seg_len:(s + 1) * seg_len], axis=1, keepdims=True)
            parts.append(jnp.broadcast_to(c, (n_e, seg_len)))
        return parts[0] if nseg == 1 else jnp.concatenate(parts, axis=1)

    def thr_step(i, thr):
        cand = thr | (jnp.int32(1) << (30 - i))
        return jnp.where(seg_count(bits >= cand) >= cap, cand, thr)

    thr = lax.fori_loop(0, 31, thr_step, jnp.zeros((n_e, n), jnp.int32))
    gt = bits > thr
    eq = bits == thr
    need = cap - seg_count(gt)

    def tie_step(i, j):
        cand = j | (jnp.int32(1) << (seg_len.bit_length() - 2 - i))
        return jnp.where(seg_count(eq & (pos < cand)) < need, cand, j)

    j = lax.fori_loop(0, seg_len.bit_length() - 1, tie_step, jnp.zeros((n_e, n), jnp.int32))
    sel = gt | (eq & (pos <= j))

    selb = jnp.where(sel, 1.0, 0.0).astype(BF)
    tiles_per_seg = seg_len // LANE
    tiles_per_chunk = MOE_TOKEN_CHUNK // LANE
    lane1 = lax.broadcasted_iota(jnp.int32, (n_e, LANE), 1)
    starts = jnp.zeros((n_e, LANE), F32)
    out = []
    run = jnp.zeros((n_e, 1), F32)
    for t in range(n // LANE):
        if t % tiles_per_seg == 0:
            run = jnp.full((n_e, 1), float((t // tiles_per_seg) * cap), F32)
        if t % tiles_per_chunk == 0:
            starts = jnp.where(lane1 == t // tiles_per_chunk, run, starts)
        inc = jnp.dot(selb[:, t * LANE:(t + 1) * LANE], tri, preferred_element_type=F32)
        out.append(inc + (run - 1.0))
        run = run + inc[:, LANE - 1:LANE]
    starts = jnp.where(lane1 == n // MOE_TOKEN_CHUNK, run, starts)
    slot = jnp.concatenate(out, axis=1).astype(jnp.int32)
    return jnp.where(sel, slot, -1), starts.astype(jnp.int32)


def _route_kernel(aff_ref, tri_ref, slot_ref, starts_ref, *, nctx, n_ctx_len, g_lo):
    n = aff_ref.shape[2]
    gb, n_e = aff_ref.shape[0], aff_ref.shape[1]
    g = pl.program_id(0) * gb + g_lo

    def run(seg_len):
        slot, starts = _route_segments(aff_ref[...].reshape(gb * n_e, n), seg_len,
                                       CAPACITY_FACTOR * seg_len // N_EXPERTS, tri_ref[...])
        slot_ref[...] = slot.reshape(gb, n_e, n)
        starts_ref[...] = starts.reshape(gb, n_e, LANE)

    if g_lo < nctx:
        pl.when(g < nctx)(functools.partial(run, n_ctx_len))
    pl.when(g >= nctx)(functools.partial(run, n))


def _route_call(aff, nctx, n_ctx_len, g_lo):
    g_n, n_e, n = aff.shape
    assert n_ctx_len % MOE_TOKEN_CHUNK == 0 and n // MOE_TOKEN_CHUNK < LANE
    gb = ROUTE_GROUPS_PER_STEP
    if (nctx - g_lo) % gb or g_n % gb:
        gb = 1
    tri = jnp.triu(jnp.ones((LANE, LANE), F32)).astype(BF)
    return pl.pallas_call(
        functools.partial(_route_kernel, nctx=nctx, n_ctx_len=n_ctx_len, g_lo=g_lo),
        grid=(g_n // gb,),
        in_specs=[pl.BlockSpec((gb, n_e, n), lambda g: (g, 0, 0)),
                  pl.BlockSpec((LANE, LANE), lambda g: (0, 0))],
        out_specs=[pl.BlockSpec((gb, n_e, n), lambda g: (g, 0, 0)),
                   pl.BlockSpec((gb, n_e, LANE), lambda g: (g, 0, 0))],
        out_shape=[jax.ShapeDtypeStruct((g_n, n_e, n), jnp.int32),
                   jax.ShapeDtypeStruct((g_n, n_e, LANE), jnp.int32)],
        compiler_params=_cparams(("parallel",)),
        name="ec_route",
    )(aff, tri)


def _gather_pairs(starts, n_slots, n_chunks):
    n_blocks = n_slots // MOE_SLOT_BLOCK
    n_pairs = n_blocks + n_chunks - 1
    assert n_blocks <= 16 and n_chunks <= 16
    s = starts[..., :n_chunks + 1]
    lo = jnp.arange(n_blocks, dtype=jnp.int32) * MOE_SLOT_BLOCK
    c_lo = jnp.sum(s[..., None, :] <= lo[:, None], axis=-1) - 1
    c_hi = jnp.sum(s[..., None, :] <= lo[:, None] + (MOE_SLOT_BLOCK - 1), axis=-1) - 1
    cnt = c_hi - c_lo + 1
    first = jnp.cumsum(cnt, axis=-1) - cnt
    j = jnp.arange(n_pairs, dtype=jnp.int32)
    r = jnp.sum(first[..., None, :] <= j[:, None], axis=-1) - 1
    pick = r[..., None] == jnp.arange(n_blocks, dtype=jnp.int32)
    c = j + jnp.sum(jnp.where(pick, (c_lo - first)[..., None, :], 0), axis=-1)
    valid = j < jnp.sum(cnt, axis=-1, keepdims=True)
    return jnp.where(valid, 256 + r * 16 + c, 0).astype(jnp.int32).reshape(-1), n_pairs


def _moe_kernel(pairs_ref, h_ref, slot_ref, gate_ref, wg_ref, wu_ref, wd_ref,
                x1_ref, g2_ref, lg_ref, lb_ref, x2_ref, xs_scr, *, n_slots, n_pairs):
    g, e = pl.program_id(0), pl.program_id(1)
    n_e = pl.num_programs(1)
    n_chunks, chunk = slot_ref.shape[1], slot_ref.shape[2]
    sb = MOE_SLOT_BLOCK

    @pl.when(e == 0)
    def _():
        x2_ref[...] = jnp.zeros_like(x2_ref)

    xs_scr[...] = jnp.zeros_like(xs_scr)
    base = (g * n_e + e) * n_pairs
    row_iota = lax.broadcasted_iota(jnp.int32, (sb, chunk), 0)
    for j in range(n_pairs):
        code = pairs_ref[base + j]
        c = code & 15
        r = (code >> 4) & 15
        rel = jnp.where(code >= 256, slot_ref[0, pl.ds(c, 1), :] - r * sb, -1)
        onehot = jnp.where(row_iota == rel, 1.0, 0.0).astype(BF)
        tok0 = pl.multiple_of(c * chunk, chunk)
        s0 = pl.multiple_of(r * sb, sb)
        xs_scr[pl.ds(s0, sb), :] += jnp.dot(onehot, h_ref[0, pl.ds(tok0, chunk), :],
                                            preferred_element_type=F32)
    xs = xs_scr[...].astype(BF)

    a = jnp.dot(xs, wg_ref[0], preferred_element_type=F32)
    u = jnp.dot(xs, wu_ref[0], preferred_element_type=F32)
    hh = (a * jax.nn.sigmoid(a) * u).astype(BF)
    y = jnp.dot(hh, wd_ref[0], preferred_element_type=F32).astype(BF)

    slot_iota = lax.broadcasted_iota(jnp.int32, (n_slots, chunk), 0)
    for c in range(n_chunks):
        rows = slice(c * chunk, (c + 1) * chunk)
        selg = jnp.where(slot_iota == slot_ref[0, c:c + 1, :], gate_ref[0, c:c + 1, :], 0.0).astype(BF)
        x2_ref[0, rows, :] += lax.dot_general(selg, y, _TN, preferred_element_type=F32)

    @pl.when(e == n_e - 1)
    def _():
        for c in range(n_chunks):
            rows = slice(c * chunk, (c + 1) * chunk)
            x2_ref[0, rows, :] = _layer_norm(DEEPNORM_ALPHA * x1_ref[0, rows, :] + g2_ref[0] * x2_ref[0, rows, :],
                                             lg_ref[...], lb_ref[...])


def _moe_call(h2, slot, starts, aff, w_gate, w_up, w_down, n_slots,
              x1, mod, layer, nctx, rows_per_layer, ln_g, ln_b, g_lo):
    g_n, n, d = h2.shape
    n_e, _, ff = w_gate.shape
    n_chunks = n // MOE_TOKEN_CHUNK
    slot3 = slot.reshape(g_n * n_e, n_chunks, MOE_TOKEN_CHUNK)
    gate3 = aff.reshape(g_n * n_e, n_chunks, MOE_TOKEN_CHUNK)
    pairs, n_pairs = _gather_pairs(starts, n_slots, n_chunks)

    def mod_map(g, e, p):
        gg = g + g_lo
        return (layer * rows_per_layer + jnp.where(gg < nctx, 0, gg - nctx + 1), 0, 5)

    full = lambda shape: pl.BlockSpec(shape, lambda g, e, p: (0,) * len(shape))
    grid_spec = pltpu.PrefetchScalarGridSpec(
        num_scalar_prefetch=1,
        grid=(g_n, n_e),
        in_specs=[
            pl.BlockSpec((1, n, d), lambda g, e, p: (g, 0, 0)),
            pl.BlockSpec((1, n_chunks, MOE_TOKEN_CHUNK), lambda g, e, p: (g * n_e + e, 0, 0)),
            pl.BlockSpec((1, n_chunks, MOE_TOKEN_CHUNK), lambda g, e, p: (g * n_e + e, 0, 0)),
            pl.BlockSpec((1, d, ff), lambda g, e, p: (e, 0, 0)),
            pl.BlockSpec((1, d, ff), lambda g, e, p: (e, 0, 0)),
            pl.BlockSpec((1, ff, d), lambda g, e, p: (e, 0, 0)),
            pl.BlockSpec((1, n, d), lambda g, e, p: (g, 0, 0)),
            pl.BlockSpec((1, 1, d), mod_map),
            full((1, d)), full((1, d)),
        ],
        out_specs=pl.BlockSpec((1, n, d), lambda g, e, p: (g, 0, 0)),
        scratch_shapes=[pltpu.VMEM((n_slots, d), F32)],
    )
    return pl.pallas_call(
        functools.partial(_moe_kernel, n_slots=n_slots, n_pairs=n_pairs),
        grid_spec=grid_spec,
        out_shape=jax.ShapeDtypeStruct((g_n, n, d), F32),
        compiler_params=_cparams(("parallel", "arbitrary")),
        name="ec_moe",
    )(pairs, h2, slot3, gate3, w_gate, w_up, w_down, x1, mod, ln_g.reshape(1, d), ln_b.reshape(1, d))


def kernel(x, c, ctx, c_ctx, ada_w, ada_b, ln_mix_g, ln_mix_b, ln_ffn_g, ln_ffn_b, router_w, expert_w_gate, expert_w_up, expert_w_down, gqa_w_qkv, gqa_q_g, gqa_k_g, gqa_w_o, mla_w_dq, mla_q_g, mla_w_uq, mla_w_dkv, mla_kv_g, mla_w_ukv, mla_w_o):
    batch, n, d = x.shape
    n_ctx_len = ctx.shape[1]
    depth = ada_w.shape[0]
    assert n % n_ctx_len == 0 and (batch * n_ctx_len) % n == 0
    nctx = batch * n_ctx_len // n
    n_slots = CAPACITY_FACTOR * n // N_EXPERTS

    rows = -(-(batch + 1) // MOD_ROWS_PAD) * MOD_ROWS_PAD
    cvec = jnp.concatenate([c_ctx[None, :], c, jnp.zeros((rows - batch - 1, d), F32)], axis=0)
    mod = _ada_call(cvec, ada_w, ada_b).reshape(depth * rows, 1, 6 * d)

    expert_w = (expert_w_gate, expert_w_up, expert_w_down)
    xs = (ctx.reshape(nctx, n, d), x)
    for i in range(depth):
        last = i == depth - 1
        j = i // N_MIXERS
        if i % N_MIXERS == 0:
            qt, k, vt = _gqa_proj_call(xs, mod, i, nctx, rows, gqa_w_qkv[j], gqa_q_g[j], gqa_k_g[j])
            ot, (w_gate, w_up, w_down) = _attn_call(qt, k, None, vt, nctx, n_ctx_len, GQA_HEADS, GQA_KV_HEADS,
                                                    not last, i, expert_w)
            w_o = gqa_w_o[j]
        else:
            qt, kn, kr, vt = _mla_proj_call(xs, mod, i, nctx, rows, mla_w_dq[j], mla_q_g[j], mla_w_uq[j],
                                            mla_w_dkv[j], mla_kv_g[j], mla_w_ukv[j])
            ot, (w_gate, w_up, w_down) = _attn_call(qt, kn, kr, vt, nctx, n_ctx_len, MLA_HEADS, MLA_HEADS,
                                                    not last, i, expert_w)
            w_o = mla_w_o[j]
        g_lo = nctx if last else 0
        x1, h2, aff = _oproj_call(ot, xs, mod, i, nctx, rows, w_o, ln_mix_g[i], ln_mix_b[i],
                                  router_w[i], g_lo)
        slot, starts = _route_call(aff, nctx, n_ctx_len, g_lo)
        xs = _moe_call(h2, slot, starts, aff, w_gate, w_up, w_down, n_slots,
                       x1, mod, i, nctx, rows, ln_ffn_g[i], ln_ffn_b[i], g_lo)
    return xs
```

```python
import functools

import jax
import jax.numpy as jnp
import numpy as np
from jax import lax
from jax.experimental import pallas as pl
from jax.experimental.pallas import tpu as pltpu

D_MODEL = 1024
DEPTH = 4
GRID_W = 64
ROPE_THETA = 10000.0
NORM_EPS = 1e-6
GQA_HEADS = 8
GQA_KV_HEADS = 2
GQA_HEAD_DIM = 128
MLA_HEADS = 8
MLA_Q_RANK = 768
MLA_KV_RANK = 256
MLA_NOPE_DIM = 128
MLA_ROPE_DIM = 64
MLA_V_DIM = 128
N_EXPERTS = 16
EXPERT_FF = 1024
CAPACITY_FACTOR = 2
N_MIXERS = 2
DEEPNORM_ALPHA = (2 * DEPTH) ** 0.25
LOG2E = 1.4426950408889634

LANE = 128
SUBLANE = 8
MOD_ROWS_PAD = 8
VMEM_LIMIT = 58 * 1024 * 1024

ADA_COLS = 1536
GQA_PROJ_ROWS, GQA_PROJ_SUB = 2048, 256
MLA_PROJ_ROWS, MLA_PROJ_SUB = 1024, 256
OPROJ_ROWS, OPROJ_SUB = 1024, 256
ATTN_Q_STEP, ATTN_Q_UNIT = 1024, 512
ATTN_KEY_CHUNK = 1024
ATTN_EXP_ROWS = 128
ROUTE_GROUPS_PER_STEP = 4
MOE_TOKEN_CHUNK = 256
MOE_SLOT_BLOCK = 128

BF = jnp.bfloat16
F32 = jnp.float32

_NT = (((1,), (1,)), ((), ()))
_TN = (((0,), (0,)), ((), ()))


def _cparams(sem):
    return pltpu.CompilerParams(dimension_semantics=sem, vmem_limit_bytes=VMEM_LIMIT)


def _ada_kernel(c_ref, w_ref, b_ref, o_ref):
    c = c_ref[...]
    s = (c * jax.nn.sigmoid(c)).astype(BF)
    o_ref[0] = jnp.dot(s, w_ref[0].astype(BF), preferred_element_type=F32) + b_ref[0]


def _ada_call(cvec, ada_w, ada_b):
    n_layers, d, n_out = ada_w.shape
    rows = cvec.shape[0]
    tn = ADA_COLS
    return pl.pallas_call(
        _ada_kernel,
        grid=(n_layers, n_out // tn),
        in_specs=[
            pl.BlockSpec((rows, d), lambda i, j: (0, 0)),
            pl.BlockSpec((1, d, tn), lambda i, j: (i, 0, j)),
            pl.BlockSpec((1, 1, tn), lambda i, j: (i, 0, j)),
        ],
        out_specs=pl.BlockSpec((1, rows, tn), lambda i, j: (i, 0, j)),
        out_shape=jax.ShapeDtypeStruct((n_layers, rows, n_out), F32),
        compiler_params=_cparams(("parallel", "parallel")),
        name="ada_mod",
    )(cvec, ada_w, ada_b.reshape(n_layers, 1, n_out))


def _rope_tables(n, rot_dim):
    t = jnp.arange(n, dtype=jnp.int32)
    row = (t // GRID_W).astype(F32)
    col = (t % GRID_W).astype(F32)
    axis_dim = rot_dim // 2
    freqs = ROPE_THETA ** (-jnp.arange(0, axis_dim, 2, dtype=F32) / axis_dim)
    ang = jnp.concatenate([row[:, None] * freqs, col[:, None] * freqs], axis=-1)
    return jnp.cos(ang), jnp.sin(ang)


def _act_specs(x, nctx, tm, g_lo=0):
    if not isinstance(x, tuple):
        return [pl.BlockSpec((1, tm, x.shape[2]), lambda g, r: (g + g_lo, r, 0))], [x]
    xc, xl = x
    if g_lo == nctx:
        return [pl.BlockSpec((1, tm, xl.shape[2]), lambda g, r: (g, r, 0))], [xl]
    assert g_lo == 0
    last_r = xc.shape[1] // tm - 1
    spec_c = pl.BlockSpec((1, tm, xc.shape[2]),
                          lambda g, r: (jnp.minimum(g, nctx - 1), jnp.where(g < nctx, r, last_r), 0))
    spec_l = pl.BlockSpec((1, tm, xl.shape[2]),
                          lambda g, r: (jnp.maximum(g - nctx, 0), jnp.where(g < nctx, 0, r), 0))
    return [spec_c, spec_l], [xc, xl]


def _act_rows(x_refs, is_ctx, rows):
    if len(x_refs) == 1:
        return x_refs[0][0, rows, :]
    return jnp.where(is_ctx, x_refs[0][0, rows, :], x_refs[1][0, rows, :])


def _gqa_proj_kernel(*refs, nctx, n_x, sub):
    x_refs, refs = refs[:n_x], refs[n_x:]
    (sh_ref, sc_ref, wqv_ref, wk_ref, gq_ref, gk_ref, cs_ref, sn_ref, cst_ref, snt_ref,
     qt_ref, k_ref, vt_ref) = refs
    is_ctx = pl.program_id(0) < nctx
    hd, half = GQA_HEAD_DIM, GQA_HEAD_DIM // 2
    n_sub = k_ref.shape[1] // sub

    def project(i):
        rows = slice(i * sub, (i + 1) * sub)
        h = (_act_rows(x_refs, is_ctx, rows) * (1.0 + sc_ref[0]) + sh_ref[0]).astype(BF)
        qv = lax.dot_general(wqv_ref[...], h, _NT, preferred_element_type=F32)
        kk = jnp.dot(h, wk_ref[...], preferred_element_type=F32)
        return qv, kk

    def finish(i, qv, kk):
        rows = slice(i * sub, (i + 1) * sub)
        cost = jnp.where(is_ctx, 1.0, cst_ref[:, rows])
        sint = jnp.where(is_ctx, 0.0, snt_ref[:, rows])
        for u in range(GQA_HEADS):
            q = qv[u * hd:(u + 1) * hd]
            ms = jnp.mean(q * q, axis=0, keepdims=True)
            qn = q * lax.rsqrt(ms + NORM_EPS) * gq_ref[...]
            x0, x1 = qn[:half], qn[half:]
            qt_ref[0, u * hd:u * hd + half, rows] = (x0 * cost - x1 * sint).astype(BF)
            qt_ref[0, u * hd + half:(u + 1) * hd, rows] = (x0 * sint + x1 * cost).astype(BF)
        vt_ref[0, :, rows] = qv[GQA_HEADS * hd:].astype(BF)
        cos = jnp.where(is_ctx, 1.0, cs_ref[rows, :])
        sin = jnp.where(is_ctx, 0.0, sn_ref[rows, :])
        for j in range(GQA_KV_HEADS):
            k = kk[:, j * hd:(j + 1) * hd]
            ms = jnp.mean(k * k, axis=1, keepdims=True)
            kn = k * lax.rsqrt(ms + NORM_EPS) * gk_ref[...]
            k_ref[0, rows, j * hd:(j + 1) * hd] = (kn * cos + pltpu.roll(kn, half, 1) * sin).astype(BF)

    prev = None
    for i in range(n_sub + 1):
        cur = project(i) if i < n_sub else None
        if i >= 1:
            finish(i - 1, *prev)
        prev = cur


def _mod_spec(layer, chunk, nctx, rows_per_layer, d):
    def imap(g, *_):
        return (layer * rows_per_layer + jnp.where(g < nctx, 0, g - nctx + 1), 0, chunk)
    return pl.BlockSpec((1, 1, d), imap)


def _gqa_proj_call(x, mod, layer, nctx, rows_per_layer, w_qkv, q_g, k_g,
                   tm=GQA_PROJ_ROWS, sub=GQA_PROJ_SUB):
    x_specs, x_args = _act_specs(x, nctx, tm)
    g_n = sum(a.shape[0] for a in x_args)
    _, n, d = x_args[0].shape
    hd, half = GQA_HEAD_DIM, GQA_HEAD_DIM // 2
    nq, nk = GQA_HEADS * hd, GQA_KV_HEADS * hd
    perm = np.concatenate([np.arange(0, hd, 2), np.arange(1, hd, 2)])
    wq = w_qkv[:, :nq].reshape(d, GQA_HEADS, hd)[:, :, perm].reshape(d, nq)
    wk = w_qkv[:, nq:nq + nk].reshape(d, GQA_KV_HEADS, hd)[:, :, perm].reshape(d, nk)
    wv = w_qkv[:, nq + nk:]
    wqv_t = jnp.concatenate([wq, wv], axis=1).T.astype(BF)
    wk = wk.astype(BF)
    gq = (q_g[perm] * (hd ** -0.5 * LOG2E)).reshape(hd, 1)
    gk = k_g[perm].reshape(1, hd)
    cos, sin = _rope_tables(n, hd)
    cs = jnp.concatenate([cos, cos], axis=1)
    sn = jnp.concatenate([-sin, sin], axis=1)
    kernel = functools.partial(_gqa_proj_kernel, nctx=nctx, n_x=len(x_args), sub=sub)
    full = lambda shape: pl.BlockSpec(shape, lambda g, r: (0,) * len(shape))
    return pl.pallas_call(
        kernel,
        grid=(g_n, n // tm),
        in_specs=x_specs + [
            _mod_spec(layer, 0, nctx, rows_per_layer, d),
            _mod_spec(layer, 1, nctx, rows_per_layer, d),
            full((nq + nk, d)),
            full((d, nk)),
            full((hd, 1)),
            full((1, hd)),
            pl.BlockSpec((tm, hd), lambda g, r: (r, 0)),
            pl.BlockSpec((tm, hd), lambda g, r: (r, 0)),
            pl.BlockSpec((half, tm), lambda g, r: (0, r)),
            pl.BlockSpec((half, tm), lambda g, r: (0, r)),
        ],
        out_specs=[
            pl.BlockSpec((1, nq, tm), lambda g, r: (g, 0, r)),
            pl.BlockSpec((1, tm, nk), lambda g, r: (g, r, 0)),
            pl.BlockSpec((1, nk, tm), lambda g, r: (g, 0, r)),
        ],
        out_shape=[
            jax.ShapeDtypeStruct((g_n, nq, n), BF),
            jax.ShapeDtypeStruct((g_n, n, nk), BF),
            jax.ShapeDtypeStruct((g_n, nk, n), BF),
        ],
        compiler_params=_cparams(("parallel", "parallel")),
        name="gqa_proj",
    )(*x_args, mod, mod, wqv_t, wk, gq, gk, cs, sn, cos.T, sin.T)


MLA_QK_PAD = 256


def _mla_proj_kernel(x_ref, sh_ref, sc_ref, wdq_ref, gq_ref, wuq_ref, wdkv_ref, gkv_ref,
                     wukn_ref, wuv_ref, cs_ref, sn_ref, cst_ref, snt_ref,
                     qt_ref, kn_ref, kr_ref, vt_ref, *, nctx, sub):
    is_ctx = pl.program_id(0) < nctx
    hr = MLA_ROPE_DIM // 2
    n_sub = kn_ref.shape[1] // sub

    def down(i):
        rows = slice(i * sub, (i + 1) * sub)
        h = (x_ref[0, rows, :] * (1.0 + sc_ref[0]) + sh_ref[0]).astype(BF)
        cq = jnp.dot(h, wdq_ref[...], preferred_element_type=F32)
        ckv = jnp.dot(h, wdkv_ref[...], preferred_element_type=F32)
        return cq, ckv

    def up(i, cq, ckv):
        rows = slice(i * sub, (i + 1) * sub)
        ms = jnp.mean(cq * cq, axis=1, keepdims=True)
        cqn = (cq * lax.rsqrt(ms + NORM_EPS) * gq_ref[...]).astype(BF)
        qt = lax.dot_general(wuq_ref[...], cqn, _NT, preferred_element_type=F32)
        c = ckv[:, :MLA_KV_RANK]
        ms = jnp.mean(c * c, axis=1, keepdims=True)
        cn = (c * lax.rsqrt(ms + NORM_EPS) * gkv_ref[...]).astype(BF)
        kr = ckv[:, MLA_KV_RANK:]
        cos = jnp.where(is_ctx, 1.0, cs_ref[rows, :])
        sin = jnp.where(is_ctx, 0.0, sn_ref[rows, :])
        kr_ref[0, rows, :] = (kr * cos + pltpu.roll(kr, LANE // 2, 1) * sin).astype(BF)
        kn_ref[0, rows, :] = jnp.dot(cn, wukn_ref[...], preferred_element_type=F32).astype(BF)
        vt_ref[0, :, rows] = lax.dot_general(wuv_ref[...], cn, _NT, preferred_element_type=F32).astype(BF)
        return qt

    def rope(i, qt):
        rows = slice(i * sub, (i + 1) * sub)
        qt = qt * ((MLA_NOPE_DIM + MLA_ROPE_DIM) ** -0.5 * LOG2E)
        cost = jnp.where(is_ctx, 1.0, cst_ref[:, rows])
        sint = jnp.where(is_ctx, 0.0, snt_ref[:, rows])
        for u in range(MLA_HEADS):
            src = u * (MLA_NOPE_DIM + MLA_ROPE_DIM)
            base = u * MLA_QK_PAD
            qt_ref[0, base:base + MLA_NOPE_DIM, rows] = qt[src:src + MLA_NOPE_DIM].astype(BF)
            r0 = base + MLA_NOPE_DIM
            x0 = qt[src + MLA_NOPE_DIM:src + MLA_NOPE_DIM + hr]
            x1 = qt[src + MLA_NOPE_DIM + hr:src + MLA_NOPE_DIM + 2 * hr]
            zero = jnp.zeros_like(x0).astype(BF)
            qt_ref[0, r0:r0 + hr, rows] = (x0 * cost - x1 * sint).astype(BF)
            qt_ref[0, r0 + hr:r0 + 2 * hr, rows] = zero
            qt_ref[0, r0 + 2 * hr:r0 + 3 * hr, rows] = (x0 * sint + x1 * cost).astype(BF)
            qt_ref[0, r0 + 3 * hr:r0 + 4 * hr, rows] = zero

    d_prev, q_prev = None, None
    for i in range(n_sub + 2):
        d_cur = down(i) if i < n_sub else None
        q_cur = up(i - 1, *d_prev) if 1 <= i <= n_sub else None
        if i >= 2:
            rope(i - 2, q_prev)
        d_prev, q_prev = d_cur, q_cur


def _mla_proj_call(x, mod, layer, nctx, rows_per_layer, w_dq, q_g, w_uq, w_dkv, kv_g, w_ukv,
                   tm=MLA_PROJ_ROWS, sub=MLA_PROJ_SUB):
    g_n, n, d = x.shape
    nh, dn, dr, dv = MLA_HEADS, MLA_NOPE_DIM, MLA_ROPE_DIM, MLA_V_DIM
    hr = dr // 2
    ev, od = np.arange(0, dr, 2), np.arange(1, dr, 2)
    wu = w_uq.reshape(MLA_Q_RANK, nh, dn + dr)
    wu = jnp.concatenate([wu[:, :, :dn], wu[:, :, dn + ev], wu[:, :, dn + od]], axis=2)
    wuq_t = wu.reshape(MLA_Q_RANK, nh * (dn + dr)).T.astype(BF)
    zk = jnp.zeros((d, hr), F32)
    wdkv = jnp.concatenate([w_dkv[:, :MLA_KV_RANK], w_dkv[:, MLA_KV_RANK + ev], zk,
                            w_dkv[:, MLA_KV_RANK + od], zk], axis=1).astype(BF)
    wkv = w_ukv.reshape(MLA_KV_RANK, nh, dn + dv)
    wukn = wkv[:, :, :dn].reshape(MLA_KV_RANK, nh * dn).astype(BF)
    wuv_t = wkv[:, :, dn:].reshape(MLA_KV_RANK, nh * dv).T.astype(BF)
    cos, sin = _rope_tables(n, dr)
    z = jnp.zeros_like(cos)
    cs = jnp.concatenate([cos, z, cos, z], axis=1)
    sn = jnp.concatenate([-sin, z, sin, z], axis=1)
    kernel = functools.partial(_mla_proj_kernel, nctx=nctx, sub=sub)
    full = lambda shape: pl.BlockSpec(shape, lambda g, r: (0,) * len(shape))
    return pl.pallas_call(
        kernel,
        grid=(g_n, n // tm),
        in_specs=[
            pl.BlockSpec((1, tm, d), lambda g, r: (g, r, 0)),
            _mod_spec(layer, 0, nctx, rows_per_layer, d),
            _mod_spec(layer, 1, nctx, rows_per_layer, d),
            full((d, MLA_Q_RANK)),
            full((1, MLA_Q_RANK)),
            full((nh * (dn + dr), MLA_Q_RANK)),
            full((d, MLA_KV_RANK + LANE)),
            full((1, MLA_KV_RANK)),
            full((MLA_KV_RANK, nh * dn)),
            full((nh * dv, MLA_KV_RANK)),
            pl.BlockSpec((tm, LANE), lambda g, r: (r, 0)),
            pl.BlockSpec((tm, LANE), lambda g, r: (r, 0)),
            pl.BlockSpec((hr, tm), lambda g, r: (0, r)),
            pl.BlockSpec((hr, tm), lambda g, r: (0, r)),
        ],
        out_specs=[
            pl.BlockSpec((1, nh * MLA_QK_PAD, tm), lambda g, r: (g, 0, r)),
            pl.BlockSpec((1, tm, nh * dn), lambda g, r: (g, r, 0)),
            pl.BlockSpec((1, tm, LANE), lambda g, r: (g, r, 0)),
            pl.BlockSpec((1, nh * dv, tm), lambda g, r: (g, 0, r)),
        ],
        out_shape=[
            jax.ShapeDtypeStruct((g_n, nh * MLA_QK_PAD, n), BF),
            jax.ShapeDtypeStruct((g_n, n, nh * dn), BF),
            jax.ShapeDtypeStruct((g_n, n, LANE), BF),
            jax.ShapeDtypeStruct((g_n, nh * dv, n), BF),
        ],
        compiler_params=_cparams(("parallel", "parallel")),
        name="mla_proj",
    )(x, mod, mod, w_dq.astype(BF), q_g.reshape(1, -1), wuq_t, wdkv, kv_g.reshape(1, -1),
      wukn, wuv_t, cs, sn, cos.T, sin.T)


def _attn_kernel(*refs, mla, with_lat, hb, rep, dq, dv, ck, sb, n_cast=0):
    refs = list(refs)
    qt_ref = refs.pop(0)
    kl_ref = refs.pop(0) if with_lat else None
    kc_ref = refs.pop(0)
    krl_ref = refs.pop(0) if (mla and with_lat) else None
    krc_ref = refs.pop(0) if mla else None
    vtl_ref = refs.pop(0) if with_lat else None
    vtc_ref = refs.pop(0)
    p_scr = refs.pop()
    s_scr = refs.pop()
    cast_out = [refs.pop() for _ in range(n_cast)][::-1]
    ot_ref = refs.pop()
    cast_in = refs[len(refs) - n_cast:]

    for w_in, w_out in zip(cast_in, cast_out):
        w_out[...] = w_in[0].astype(BF)

    tq = s_scr.shape[2]

    def stage_a(t, unit, ci):
        h, q0, chunks = unit
        k_ref, kr_ref, _, r0, rows, s0 = chunks[ci]
        j = h // rep
        keys = k_ref[0, r0:r0 + rows, j * LANE:(j + 1) * LANE]
        if mla:
            keys = jnp.concatenate([keys, kr_ref[0, r0:r0 + rows, :]], axis=1)
        s = jnp.dot(keys, qt_ref[0, h * dq:(h + 1) * dq, q0:q0 + tq],
                    preferred_element_type=F32)
        s_scr[t % 2, s0:s0 + rows, :] = s
        return jnp.max(s.reshape(rows // SUBLANE, SUBLANE, tq), axis=0)

    def stage_b(t, unit, ci, m):
        h, _, chunks = unit
        _, _, vt_ref, r0, rows, s0 = chunks[ci]
        j = h // rep
        l8 = None
        for r in range(0, rows, sb):
            p = jnp.exp2(s_scr[t % 2, s0 + r:s0 + r + sb, :] - m)
            ps = jnp.sum(p.reshape(sb // SUBLANE, SUBLANE, tq), axis=0)
            l8 = ps if l8 is None else l8 + ps
            p_scr[ci % 2, r:r + sb, :] = p.astype(BF)
        pv = jnp.dot(vt_ref[0, j * dv:(j + 1) * dv, r0:r0 + rows], p_scr[ci % 2, :rows, :],
                     preferred_element_type=F32)
        return l8, pv

    if with_lat:
        n_ctx = kc_ref.shape[1]
        chunks = [(kc_ref, krc_ref, vtc_ref, 0, n_ctx, 0)]
        for r0 in range(0, kl_ref.shape[1], ck):
            chunks.append((kl_ref, krl_ref, vtl_ref, r0, ck, n_ctx + r0))
        units = [(h, q0, chunks) for q0 in range(0, qt_ref.shape[2], tq) for h in range(hb)]
    else:
        units = [(h, s * tq, [(kc_ref, krc_ref, vtc_ref, s * tq, tq, 0)])
                 for s in range(kc_ref.shape[1] // tq) for h in range(hb)]

    m_cur = None
    for t in range(len(units) + 1):
        m_parts, l_parts, acc = [], [], None
        for ci in range(len(units[0][2])):
            if t < len(units):
                m_parts.append(stage_a(t, units[t], ci))
            if t >= 1:
                l8, pv = stage_b(t - 1, units[t - 1], ci, m_cur)
                l_parts.append(l8)
                acc = pv if acc is None else acc + pv
        if t >= 1:
            h, q0, _ = units[t - 1]
            l = jnp.sum(functools.reduce(jnp.add, l_parts), axis=0, keepdims=True)
            ot_ref[0, h * dv:(h + 1) * dv, q0:q0 + tq] = (acc / l).astype(BF)
        if t < len(units):
            m_cur = jnp.max(functools.reduce(jnp.maximum, m_parts), axis=0, keepdims=True)


def _attn_call(qt, k, kr, vt, nctx, n_ctx_len, n_heads, n_kv_heads, with_ctx_out, layer, expert_weights,
               tq=ATTN_Q_UNIT, tqb=ATTN_Q_STEP, ck=ATTN_KEY_CHUNK, sb=ATTN_EXP_ROWS):
    g_n, _, n = qt.shape
    dq = qt.shape[1] // n_heads
    dv = vt.shape[1] // n_kv_heads
    batch = g_n - nctx
    spg = n // n_ctx_len
    rep = n_heads // n_kv_heads
    hb = n_heads
    hkv = hb // rep
    mla = kr is not None
    ot_shape = jax.ShapeDtypeStruct((g_n, n_heads * dv, n), BF)
    kern = functools.partial(_attn_kernel, mla=mla, hb=hb, rep=rep, dq=dq, dv=dv, ck=ck, sb=sb)

    lat_in = [pl.BlockSpec((1, hb * dq, tqb), lambda b, h, t: (nctx + b, h, t)),
              pl.BlockSpec((1, n, hkv * LANE), lambda b, h, t: (nctx + b, 0, h)),
              pl.BlockSpec((1, n_ctx_len, hkv * LANE), lambda b, h, t: (b // spg, b % spg, h))]
    lat_args = [qt, k, k]
    if mla:
        lat_in += [pl.BlockSpec((1, n, LANE), lambda b, h, t: (nctx + b, 0, 0)),
                   pl.BlockSpec((1, n_ctx_len, LANE), lambda b, h, t: (b // spg, b % spg, 0))]
        lat_args += [kr, kr]
    lat_in += [pl.BlockSpec((1, hkv * dv, n), lambda b, h, t: (nctx + b, h, 0)),
               pl.BlockSpec((1, hkv * dv, n_ctx_len), lambda b, h, t: (b // spg, h, b % spg))]
    lat_args += [vt, vt]
    n_steps = batch * (n // tqb)
    n_t = n // tqb
    cast_out_specs, cast_out_shapes = [], []
    for w in expert_weights:
        n_l, n_e, a, b_ = w.shape
        assert (n_e * a) % n_steps == 0 and (n_e * a // n_steps) % (2 * SUBLANE) == 0
        slab = n_e * a // n_steps
        lat_in.append(pl.BlockSpec((1, slab, b_), lambda b, h, t: (layer, b * n_t + t, 0)))
        lat_args.append(w.reshape(n_l, n_e * a, b_))
        cast_out_specs.append(pl.BlockSpec((slab, b_), lambda b, h, t: (b * n_t + t, 0)))
        cast_out_shapes.append(jax.ShapeDtypeStruct((n_e * a, b_), BF))
    ot, *w_bf = pl.pallas_call(
        functools.partial(kern, with_lat=True, n_cast=len(expert_weights)),
        grid=(batch, n_heads // hb, n_t),
        in_specs=lat_in,
        out_specs=[pl.BlockSpec((1, hb * dv, tqb), lambda b, h, t: (nctx + b, h, t))] + cast_out_specs,
        out_shape=[ot_shape] + cast_out_shapes,
        scratch_shapes=[pltpu.VMEM((2, n_ctx_len + n, tq), F32),
                        pltpu.VMEM((2, max(ck, n_ctx_len), tq), BF)],
        compiler_params=_cparams(("parallel", "parallel", "parallel")),
        name="attn_latent",
    )(*lat_args)
    w_bf = [wb.reshape(w.shape[1:]) for wb, w in zip(w_bf, expert_weights)]
    if not with_ctx_out:
        return ot, w_bf

    ctx_in = [pl.BlockSpec((1, hb * dq, n), lambda g: (g, 0, 0)),
              pl.BlockSpec((1, n, hkv * LANE), lambda g: (g, 0, 0))]
    ctx_args = [qt, k]
    if mla:
        ctx_in += [pl.BlockSpec((1, n, LANE), lambda g: (g, 0, 0))]
        ctx_args += [kr]
    ctx_in += [pl.BlockSpec((1, hkv * dv, n), lambda g: (g, 0, 0)),
               pl.BlockSpec(memory_space=pl.ANY)]
    ctx_args += [vt, ot]
    ot = pl.pallas_call(
        functools.partial(kern, with_lat=False),
        grid=(nctx,),
        in_specs=ctx_in,
        out_specs=pl.BlockSpec((1, hb * dv, n), lambda g: (g, 0, 0)),
        out_shape=ot_shape,
        input_output_aliases={len(ctx_args) - 1: 0},
        scratch_shapes=[pltpu.VMEM((2, n_ctx_len, n_ctx_len), F32),
                        pltpu.VMEM((2, n_ctx_len, n_ctx_len), BF)],
        compiler_params=_cparams(("parallel",)),
        name="attn_ctx",
    )(*ctx_args)
    return ot, w_bf


def _deepnorm_ln(x, gate, y, g, b):
    z = x + (gate * (1.0 / DEEPNORM_ALPHA)) * y
    mu = jnp.mean(z, axis=-1, keepdims=True)
    zc = z - mu
    var = jnp.mean(zc * zc, axis=-1, keepdims=True)
    return zc * lax.rsqrt(var + NORM_EPS / DEEPNORM_ALPHA ** 2) * g + b


def _oproj_kernel(*refs, nctx, g_lo, n_x, sub):
    x_refs, refs = refs[:n_x], refs[n_x:]
    (ot_ref, wo_ref, g1_ref, lg_ref, lb_ref, sh2_ref, sc2_ref, wr_ref,
     x1_ref, h2_ref, aff_ref) = refs
    is_ctx = pl.program_id(0) + g_lo < nctx
    tm = x1_ref.shape[1]

    def project(i):
        return lax.dot_general(ot_ref[0, :, i * sub:(i + 1) * sub], wo_ref[...], _TN,
                               preferred_element_type=F32)

    def finish(i, y):
        rows = slice(i * sub, (i + 1) * sub)
        x = _act_rows(x_refs, is_ctx, rows)
        x1 = _deepnorm_ln(x, g1_ref[0], y, lg_ref[...], lb_ref[...])
        x1_ref[0, rows, :] = x1
        h2 = x1 * (1.0 + sc2_ref[0]) + sh2_ref[0]
        hb = h2.astype(BF)
        h2_ref[0, rows, :] = hb
        lg = lax.dot_general(wr_ref[...], hb, _NT, preferred_element_type=F32)
        e = jnp.exp(lg - jnp.max(lg, axis=0, keepdims=True))
        aff_ref[0, :, rows] = e / jnp.sum(e, axis=0, keepdims=True)

    y_prev = None
    for i in range(tm // sub + 1):
        y = project(i) if i < tm // sub else None
        if i >= 1:
            finish(i - 1, y_prev)
        y_prev = y


def _oproj_call(ot, x, mod, layer, nctx, rows_per_layer, w_o, ln_g, ln_b, w_router, g_lo,
                tm=OPROJ_ROWS, sub=OPROJ_SUB):
    x_specs, x_args = _act_specs(x, nctx, tm, g_lo)
    g_n = ot.shape[0]
    _, n, d = x_args[0].shape
    n_e = w_router.shape[1]
    wr_t = w_router.T.astype(BF)
    full = lambda shape: pl.BlockSpec(shape, lambda g, r: (0,) * len(shape))

    def mspec(chunk):
        def imap(g, r):
            gg = g + g_lo
            return (layer * rows_per_layer + jnp.where(gg < nctx, 0, gg - nctx + 1), 0, chunk)
        return pl.BlockSpec((1, 1, d), imap)

    return pl.pallas_call(
        functools.partial(_oproj_kernel, nctx=nctx, g_lo=g_lo, n_x=len(x_args), sub=sub),
        grid=(g_n - g_lo, n // tm),
        in_specs=x_specs + [
            pl.BlockSpec((1, ot.shape[1], tm), lambda g, r: (g + g_lo, 0, r)),
            full(w_o.shape),
            mspec(2),
            full((1, d)), full((1, d)),
            mspec(3), mspec(4),
            full((n_e, d)),
        ],
        out_specs=[
            pl.BlockSpec((1, tm, d), lambda g, r: (g, r, 0)),
            pl.BlockSpec((1, tm, d), lambda g, r: (g, r, 0)),
            pl.BlockSpec((1, n_e, tm), lambda g, r: (g, 0, r)),
        ],
        out_shape=[
            jax.ShapeDtypeStruct((g_n - g_lo, n, d), F32),
            jax.ShapeDtypeStruct((g_n - g_lo, n, d), BF),
            jax.ShapeDtypeStruct((g_n - g_lo, n_e, n), F32),
        ],
        compiler_params=_cparams(("parallel", "parallel")),
        name="oproj_norm_router",
    )(*x_args, ot, w_o.astype(BF), mod, ln_g.reshape(1, d), ln_b.reshape(1, d), mod, mod, wr_t)


def _route_segments(aff, seg_len, cap, tri):
    n_e, n = aff.shape
    nseg = n // seg_len
    bits = pltpu.bitcast(aff, jnp.int32)
    lane = lax.broadcasted_iota(jnp.int32, (n_e, n), 1)
    assert seg_len & (seg_len - 1) == 0
    pos = lane & (seg_len - 1)

    def seg_count(mask):
        v = jnp.where(mask, 1.0, 0.0)
        parts = []
        for s in range(nseg):
            c = jnp.sum(v[:, s * seg_len:(s + 1) * seg_len], axis=1, keepdims=True)
            parts.append(jnp.broadcast_to(c, (n_e, seg_len)))
        return parts[0] if nseg == 1 else jnp.concatenate(parts, axis=1)

    def thr_step(i, thr):
        cand = thr | (jnp.int32(1) << (30 - i))
        return jnp.where(seg_count(bits >= cand) >= cap, cand, thr)

    thr = lax.fori_loop(0, 31, thr_step, jnp.zeros((n_e, n), jnp.int32))
    gt = bits > thr
    eq = bits == thr
    need = cap - seg_count(gt)

    def tie_step(i, j):
        cand = j | (jnp.int32(1) << (seg_len.bit_length() - 2 - i))
        return jnp.where(seg_count(eq & (pos < cand)) < need, cand, j)

    j = lax.fori_loop(0, seg_len.bit_length() - 1, tie_step, jnp.zeros((n_e, n), jnp.int32))
    sel = gt | (eq & (pos <= j))

    selb = jnp.where(sel, 1.0, 0.0).astype(BF)
    tiles_per_seg = seg_len // LANE
    tiles_per_chunk = MOE_TOKEN_CHUNK // LANE
    lane1 = lax.broadcasted_iota(jnp.int32, (n_e, LANE), 1)
    starts = jnp.zeros((n_e, LANE), F32)
    out = []
    run = jnp.zeros((n_e, 1), F32)
    for t in range(n // LANE):
        if t % tiles_per_seg == 0:
            run = jnp.full((n_e, 1), float((t // tiles_per_seg) * cap), F32)
        if t % tiles_per_chunk == 0:
            starts = jnp.where(lane1 == t // tiles_per_chunk, run, starts)
        inc = jnp.dot(selb[:, t * LANE:(t + 1) * LANE], tri, preferred_element_type=F32)
        out.append(inc + (run - 1.0))
        run = run + inc[:, LANE - 1:LANE]
    starts = jnp.where(lane1 == n // MOE_TOKEN_CHUNK, run, starts)
    slot = jnp.concatenate(out, axis=1).astype(jnp.int32)
    return jnp.where(sel, slot, -1), starts.astype(jnp.int32)


def _route_kernel(aff_ref, tri_ref, slot_ref, starts_ref, *, nctx, n_ctx_len, g_lo):
    n = aff_ref.shape[2]
    gb, n_e = aff_ref.shape[0], aff_ref.shape[1]
    g = pl.program_id(0) * gb + g_lo

    def run(seg_len):
        slot, starts = _route_segments(aff_ref[...].reshape(gb * n_e, n), seg_len,
                                       CAPACITY_FACTOR * seg_len // N_EXPERTS, tri_ref[...])
        slot_ref[...] = slot.reshape(gb, n_e, n)
        starts_ref[...] = starts.reshape(gb, n_e, LANE)

    if g_lo < nctx:
        pl.when(g < nctx)(functools.partial(run, n_ctx_len))
    pl.when(g >= nctx)(functools.partial(run, n))


def _route_call(aff, nctx, n_ctx_len, g_lo):
    g_n, n_e, n = aff.shape
    assert n_ctx_len % MOE_TOKEN_CHUNK == 0 and n // MOE_TOKEN_CHUNK < LANE
    gb = ROUTE_GROUPS_PER_STEP
    if (nctx - g_lo) % gb or g_n % gb:
        gb = 1
    tri = jnp.triu(jnp.ones((LANE, LANE), F32)).astype(BF)
    return pl.pallas_call(
        functools.partial(_route_kernel, nctx=nctx, n_ctx_len=n_ctx_len, g_lo=g_lo),
        grid=(g_n // gb,),
        in_specs=[pl.BlockSpec((gb, n_e, n), lambda g: (g, 0, 0)),
                  pl.BlockSpec((LANE, LANE), lambda g: (0, 0))],
        out_specs=[pl.BlockSpec((gb, n_e, n), lambda g: (g, 0, 0)),
                   pl.BlockSpec((gb, n_e, LANE), lambda g: (g, 0, 0))],
        out_shape=[jax.ShapeDtypeStruct((g_n, n_e, n), jnp.int32),
                   jax.ShapeDtypeStruct((g_n, n_e, LANE), jnp.int32)],
        compiler_params=_cparams(("parallel",)),
        name="ec_route",
    )(aff, tri)


def _gather_pairs(starts, n_slots, n_chunks):
    n_blocks = n_slots // MOE_SLOT_BLOCK
    n_pairs = n_blocks + n_chunks - 1
    assert n_blocks <= 16 and n_chunks <= 16
    s = starts[..., :n_chunks + 1]
    lo = jnp.arange(n_blocks, dtype=jnp.int32) * MOE_SLOT_BLOCK
    c_lo = jnp.sum(s[..., None, :] <= lo[:, None], axis=-1) - 1
    c_hi = jnp.sum(s[..., None, :] <= lo[:, None] + (MOE_SLOT_BLOCK - 1), axis=-1) - 1
    cnt = c_hi - c_lo + 1
    first = jnp.cumsum(cnt, axis=-1) - cnt
    j = jnp.arange(n_pairs, dtype=jnp.int32)
    r = jnp.sum(first[..., None, :] <= j[:, None], axis=-1) - 1
    pick = r[..., None] == jnp.arange(n_blocks, dtype=jnp.int32)
    c = j + jnp.sum(jnp.where(pick, (c_lo - first)[..., None, :], 0), axis=-1)
    valid = j < jnp.sum(cnt, axis=-1, keepdims=True)
    return jnp.where(valid, 256 + r * 16 + c, 0).astype(jnp.int32).reshape(-1), n_pairs


def _moe_kernel(pairs_ref, h_ref, slot_ref, gate_ref, wg_ref, wu_ref, wd_ref,
                x1_ref, g2_ref, lg_ref, lb_ref, x2_ref, xs_scr, *, n_slots, n_pairs):
    g, e = pl.program_id(0), pl.program_id(1)
    n_e = pl.num_programs(1)
    n_chunks, chunk = slot_ref.shape[1], slot_ref.shape[2]
    sb = MOE_SLOT_BLOCK

    @pl.when(e == 0)
    def _():
        x2_ref[...] = jnp.zeros_like(x2_ref)

    xs_scr[...] = jnp.zeros_like(xs_scr)
    base = (g * n_e + e) * n_pairs
    row_iota = lax.broadcasted_iota(jnp.int32, (sb, chunk), 0)
    for j in range(n_pairs):
        code = pairs_ref[base + j]
        c = code & 15
        r = (code >> 4) & 15
        rel = jnp.where(code >= 256, slot_ref[0, pl.ds(c, 1), :] - r * sb, -1)
        onehot = jnp.where(row_iota == rel, 1.0, 0.0).astype(BF)
        tok0 = pl.multiple_of(c * chunk, chunk)
        s0 = pl.multiple_of(r * sb, sb)
        xs_scr[pl.ds(s0, sb), :] += jnp.dot(onehot, h_ref[0, pl.ds(tok0, chunk), :],
                                            preferred_element_type=F32)
    xs = xs_scr[...].astype(BF)

    a = jnp.dot(xs, wg_ref[0], preferred_element_type=F32)
    u = jnp.dot(xs, wu_ref[0], preferred_element_type=F32)
    hh = (a * jax.nn.sigmoid(a) * u).astype(BF)
    y = jnp.dot(hh, wd_ref[0], preferred_element_type=F32).astype(BF)

    slot_iota = lax.broadcasted_iota(jnp.int32, (n_slots, chunk), 0)
    for c in range(n_chunks):
        rows = slice(c * chunk, (c + 1) * chunk)
        selg = jnp.where(slot_iota == slot_ref[0, c:c + 1, :], gate_ref[0, c:c + 1, :], 0.0).astype(BF)
        x2_ref[0, rows, :] += lax.dot_general(selg, y, _TN, preferred_element_type=F32)

    @pl.when(e == n_e - 1)
    def _():
        for c in range(n_chunks):
            rows = slice(c * chunk, (c + 1) * chunk)
            x2_ref[0, rows, :] = _deepnorm_ln(x1_ref[0, rows, :], g2_ref[0], x2_ref[0, rows, :],
                                              lg_ref[...], lb_ref[...])


def _moe_call(h2, slot, starts, aff, w_gate, w_up, w_down, n_slots,
              x1, mod, layer, nctx, rows_per_layer, ln_g, ln_b, g_lo):
    g_n, n, d = h2.shape
    n_e, _, ff = w_gate.shape
    n_chunks = n // MOE_TOKEN_CHUNK
    slot3 = slot.reshape(g_n * n_e, n_chunks, MOE_TOKEN_CHUNK)
    gate3 = aff.reshape(g_n * n_e, n_chunks, MOE_TOKEN_CHUNK)
    pairs, n_pairs = _gather_pairs(starts, n_slots, n_chunks)

    def mod_map(g, e, p):
        gg = g + g_lo
        return (layer * rows_per_layer + jnp.where(gg < nctx, 0, gg - nctx + 1), 0, 5)

    full = lambda shape: pl.BlockSpec(shape, lambda g, e, p: (0,) * len(shape))
    grid_spec = pltpu.PrefetchScalarGridSpec(
        num_scalar_prefetch=1,
        grid=(g_n, n_e),
        in_specs=[
            pl.BlockSpec((1, n, d), lambda g, e, p: (g, 0, 0)),
            pl.BlockSpec((1, n_chunks, MOE_TOKEN_CHUNK), lambda g, e, p: (g * n_e + e, 0, 0)),
            pl.BlockSpec((1, n_chunks, MOE_TOKEN_CHUNK), lambda g, e, p: (g * n_e + e, 0, 0)),
            pl.BlockSpec((1, d, ff), lambda g, e, p: (e, 0, 0)),
            pl.BlockSpec((1, d, ff), lambda g, e, p: (e, 0, 0)),
            pl.BlockSpec((1, ff, d), lambda g, e, p: (e, 0, 0)),
            pl.BlockSpec((1, n, d), lambda g, e, p: (g, 0, 0)),
            pl.BlockSpec((1, 1, d), mod_map),
            full((1, d)), full((1, d)),
        ],
        out_specs=pl.BlockSpec((1, n, d), lambda g, e, p: (g, 0, 0)),
        scratch_shapes=[pltpu.VMEM((n_slots, d), F32)],
    )
    return pl.pallas_call(
        functools.partial(_moe_kernel, n_slots=n_slots, n_pairs=n_pairs),
        grid_spec=grid_spec,
        out_shape=jax.ShapeDtypeStruct((g_n, n, d), F32),
        compiler_params=_cparams(("parallel", "arbitrary")),
        name="ec_moe",
    )(pairs, h2, slot3, gate3, w_gate, w_up, w_down, x1, mod, ln_g.reshape(1, d), ln_b.reshape(1, d))


def kernel(x, c, ctx, c_ctx, ada_w, ada_b, ln_mix_g, ln_mix_b, ln_ffn_g, ln_ffn_b, router_w, expert_w_gate, expert_w_up, expert_w_down, gqa_w_qkv, gqa_q_g, gqa_k_g, gqa_w_o, mla_w_dq, mla_q_g, mla_w_uq, mla_w_dkv, mla_kv_g, mla_w_ukv, mla_w_o):
    batch, n, d = x.shape
    n_ctx_len = ctx.shape[1]
    depth = ada_w.shape[0]
    assert n % n_ctx_len == 0 and (batch * n_ctx_len) % n == 0
    nctx = batch * n_ctx_len // n
    n_slots = CAPACITY_FACTOR * n // N_EXPERTS

    rows = -(-(batch + 1) // MOD_ROWS_PAD) * MOD_ROWS_PAD
    cvec = jnp.concatenate([c_ctx[None, :], c, jnp.zeros((rows - batch - 1, d), F32)], axis=0)
    mod = _ada_call(cvec, ada_w, ada_b).reshape(depth * rows, 1, 6 * d)

    expert_w = (expert_w_gate, expert_w_up, expert_w_down)
    xs = (ctx.reshape(nctx, n, d), x)
    for i in range(depth):
        last = i == depth - 1
        j = i // N_MIXERS
        if i % N_MIXERS == 0:
            qt, k, vt = _gqa_proj_call(xs, mod, i, nctx, rows, gqa_w_qkv[j], gqa_q_g[j], gqa_k_g[j])
            ot, (w_gate, w_up, w_down) = _attn_call(qt, k, None, vt, nctx, n_ctx_len, GQA_HEADS, GQA_KV_HEADS,
                                                    not last, i, expert_w)
            w_o = gqa_w_o[j]
        else:
            qt, kn, kr, vt = _mla_proj_call(xs, mod, i, nctx, rows, mla_w_dq[j], mla_q_g[j], mla_w_uq[j],
                                            mla_w_dkv[j], mla_kv_g[j], mla_w_ukv[j])
            ot, (w_gate, w_up, w_down) = _attn_call(qt, kn, kr, vt, nctx, n_ctx_len, MLA_HEADS, MLA_HEADS,
                                                    not last, i, expert_w)
            w_o = mla_w_o[j]
        g_lo = nctx if last else 0
        x1, h2, aff = _oproj_call(ot, xs, mod, i, nctx, rows, w_o, ln_mix_g[i], ln_mix_b[i],
                                  router_w[i], g_lo)
        slot, starts = _route_call(aff, nctx, n_ctx_len, g_lo)
        xs = _moe_call(h2, slot, starts, aff, w_gate, w_up, w_down, n_slots,
                       x1, mod, i, nctx, rows, ln_ffn_g[i], ln_ffn_b[i], g_lo)
    return xs
```
